```python
import jax, jax.numpy as jnp
from jax import lax
import numpy as np

D_MODEL = 1024
BATCH = 2
SEQ = 8192
DEPTH = 1
DEC_BATCH = 128
DEC_SEQ = 1
PAST_LEN = 8192
PAGE_SIZE = 128

HEAD_DIM = 128
HEADS_PER_GROUP = 4
DILATED_GROUPS = ((128, 1), (512, 4), (2048, 16))
N_DGROUPS = len(DILATED_GROUPS)
N_HEADS = HEADS_PER_GROUP * N_DGROUPS
ATTN_W = N_HEADS * HEAD_DIM
ATTN_OUT = HEADS_PER_GROUP * HEAD_DIM
SUB_BLOCK = 128
CONV_CH = D_MODEL
CONV_WIDTH = 31
N_COARSE = 4
EXPERTS_PER_GROUP = 8
N_EXPERTS = N_COARSE * EXPERTS_PER_GROUP
TOP_K_FINE = 2
D_EXPERT = 256
IN_W = 3 * ATTN_W + 2 * CONV_CH + 2 * D_MODEL
EPS = 1e-6
NEG = -1e30

kernel_name = "hybrid_dilated_conformer_hmoe_step"


def rmsnorm(x, g):
    xf = x.astype(jnp.float32)
    r = lax.rsqrt(jnp.mean(xf * xf, axis=-1, keepdims=True) + EPS)
    return (xf * r).astype(x.dtype) * g


def mixer_inputs(xn, w_in):
    p = xn @ w_in
    cuts = (ATTN_W, 2 * ATTN_W, 3 * ATTN_W, 3 * ATTN_W + CONV_CH, 3 * ATTN_W + 2 * CONV_CH,
            3 * ATTN_W + 2 * CONV_CH + D_MODEL)
    q, k, v, a, b, ga, gb = jnp.split(p, cuts, axis=-1)
    shp = xn.shape[:-1] + (N_DGROUPS, HEADS_PER_GROUP, HEAD_DIM)
    u = a * jax.nn.sigmoid(b)
    return q.reshape(shp), k.reshape(shp), v.reshape(shp), u, ga, gb


def attend(s, valid, v, eq):
    s = jnp.where(valid, s, NEG)
    m = jnp.max(s, axis=-1, keepdims=True)
    p = jnp.exp(s - m)
    den = jnp.sum(p, axis=-1, keepdims=True)
    o = jnp.einsum(eq, (p / den).astype(v.dtype), v)
    lse = (m + jnp.log(den))[..., 0]
    return o, lse


def dilated_attn_prompt(q, k, v, dil, span):
    B, S, H, hd = q.shape
    L = S // dil
    nb = -(-L // SUB_BLOCK)
    Lp = nb * SUB_BLOCK
    N = B * dil

    def residue_major(t):
        t = t.reshape(B, L, dil, H, hd).transpose(0, 2, 1, 3, 4).reshape(N, L, H, hd)
        return jnp.pad(t, ((0, 0), (0, Lp - L), (0, 0), (0, 0)))

    qs, ks, vs = residue_major(q), residue_major(k), residue_major(v)

    def banded(t):
        cur = t.reshape(N, nb, SUB_BLOCK, H, hd)
        prev = jnp.pad(t, ((0, 0), (SUB_BLOCK, 0), (0, 0), (0, 0)))[:, :Lp].reshape(N, nb, SUB_BLOCK, H, hd)
        return jnp.concatenate([prev, cur], axis=2)

    kb, vb = banded(ks), banded(vs)
    qb = qs.reshape(N, nb, SUB_BLOCK, H, hd)
    s = jnp.einsum('nbqhd,nbkhd->nbhqk', qb, kb).astype(jnp.float32) * (hd ** -0.5)
    qi = jnp.arange(SUB_BLOCK)[:, None] + SUB_BLOCK
    kj = jnp.arange(2 * SUB_BLOCK)[None, :]
    dist = qi - kj
    kpos = jnp.arange(nb)[:, None, None] * SUB_BLOCK + kj[None] - SUB_BLOCK
    valid = (dist >= 0) & (dist <= span) & (kpos >= 0)
    o, lse = attend(s, valid[None, :, None], vb, 'nbhqk,nbkhd->nbqhd')
    o = o.reshape(N, Lp, H, hd)[:, :L].reshape(B, dil, L, H, hd).transpose(0, 2, 1, 3, 4).reshape(B, S, H, hd)
    lse = lse.transpose(0, 1, 3, 2).reshape(N, Lp, H)[:, :L].reshape(B, dil, L, H).transpose(0, 2, 1, 3).reshape(B, S, H)
    return o, lse


def dilated_attn_sample(q, kall, vall, dil, span):
    T = q.shape[1]
    Wb = kall.shape[1] - T
    idx = Wb + jnp.arange(T)[:, None] - dil * jnp.arange(span + 1)[None, :]
    valid = idx >= 0
    idx = jnp.maximum(idx, 0)
    kg = kall[:, idx]
    vg = vall[:, idx]
    s = jnp.einsum('bthd,btjhd->bhtj', q, kg).astype(jnp.float32) * (q.shape[-1] ** -0.5)
    o, lse = attend(s, valid[None, None], vg, 'bhtj,btjhd->bthd')
    return o, lse.transpose(0, 2, 1)


def combine_groups(outs, lses):
    w = jax.nn.softmax(jnp.stack(lses, axis=0), axis=0)
    o = jnp.sum(w[..., None] * jnp.stack(outs, axis=0).astype(jnp.float32), axis=0)
    return o.reshape(o.shape[:-2] + (ATTN_OUT,)).astype(outs[0].dtype)


def conv_branch(u_hist, w_dw, b_dw, cn_g, cn_b):
    c = lax.conv_general_dilated(u_hist, w_dw[:, None, :], window_strides=(1,), padding='VALID',
                                 dimension_numbers=('NWC', 'WIO', 'NWC'),
                                 feature_group_count=CONV_CH) + b_dw
    cf = c.astype(jnp.float32)
    mu = jnp.mean(cf, axis=-1, keepdims=True)
    var = jnp.mean(jnp.square(cf - mu), axis=-1, keepdims=True)
    cn = ((cf - mu) * lax.rsqrt(var + EPS)).astype(c.dtype) * cn_g + cn_b
    return jax.nn.silu(cn)


def merge(o_attn, c, ga, gb, w_branch, w_out):
    ya = o_attn @ w_branch[:ATTN_OUT]
    yb = c @ w_branch[ATTN_OUT:]
    return (jax.nn.sigmoid(ga) * ya + jax.nn.sigmoid(gb) * yb) @ w_out


def mixer_prompt(xn, w_in, w_dw, b_dw, cn_g, cn_b, w_branch, w_out):
    q, k, v, u, ga, gb = mixer_inputs(xn, w_in)
    S = xn.shape[1]
    outs, lses, kv_states = [], [], []
    for g, (win, dil) in enumerate(DILATED_GROUPS):
        o, l = dilated_attn_prompt(q[:, :, g], k[:, :, g], v[:, :, g], dil, win // dil)
        outs.append(o)
        lses.append(l)
        keep = min(win, S)
        kv_states.append(jnp.stack([k[:, S - keep:, g], v[:, S - keep:, g]], axis=2))
    o_attn = combine_groups(outs, lses)
    c = conv_branch(jnp.pad(u, ((0, 0), (CONV_WIDTH - 1, 0), (0, 0))), w_dw, b_dw, cn_g, cn_b)
    h = merge(o_attn, c, ga, gb, w_branch, w_out)
    return h, kv_states, u[:, S - (CONV_WIDTH - 1):]


def mixer_sample(xn, kv_caches, conv_buf, w_in, w_dw, b_dw, cn_g, cn_b, w_branch, w_out):
    q, k, v, u, ga, gb = mixer_inputs(xn, w_in)
    T = xn.shape[1]
    outs, lses, kv_states = [], [], []
    for g, (win, dil) in enumerate(DILATED_GROUPS):
        kv_all = jnp.concatenate([kv_caches[g].astype(k.dtype), jnp.stack([k[:, :, g], v[:, :, g]], axis=2)], axis=1)
        o, l = dilated_attn_sample(q[:, :, g], kv_all[:, :, 0], kv_all[:, :, 1], dil, win // dil)
        outs.append(o)
        lses.append(l)
        kv_states.append(kv_all[:, T:])
    o_attn = combine_groups(outs, lses)
    uh = jnp.concatenate([conv_buf.astype(u.dtype), u], axis=1)
    c = conv_branch(uh, w_dw, b_dw, cn_g, cn_b)
    h = merge(o_attn, c, ga, gb, w_branch, w_out)
    return h, kv_states, uh[:, T:]


def hier_moe(x, w_rg, b_rg, w_re, b_re, w_eg, w_eu, w_ed):
    shp = x.shape
    t = x.reshape(-1, D_MODEL)
    n = t.shape[0]
    glog = (t @ w_rg + b_rg).astype(jnp.float32)
    gsel = jnp.argmax(glog, axis=-1)
    gw = jnp.max(jax.nn.softmax(glog, axis=-1), axis=-1)
    elog = (t @ w_re + b_re).astype(jnp.float32).reshape(n, N_COARSE, EXPERTS_PER_GROUP)
    esel = jnp.take_along_axis(elog, gsel[:, None, None], axis=1)[:, 0]
    tv, ti = lax.top_k(esel, TOP_K_FINE)
    tw = jax.nn.softmax(tv, axis=-1) * gw[:, None]
    eid = gsel[:, None] * EXPERTS_PER_GROUP + ti
    cw = jnp.sum(jax.nn.one_hot(eid, N_EXPERTS, dtype=jnp.float32) * tw[..., None], axis=1)
    y = jnp.zeros((n, D_MODEL), jnp.float32)
    for g in range(N_COARSE):
        sl = slice(g * EXPERTS_PER_GROUP, (g + 1) * EXPERTS_PER_GROUP)
        hg = jax.nn.silu(jnp.einsum('nd,edf->nef', t, w_eg[sl])) * jnp.einsum('nd,edf->nef', t, w_eu[sl])
        hg = hg * cw[:, sl, None].astype(hg.dtype)
        y = y + jnp.einsum('nef,efd->nd', hg, w_ed[sl]).astype(jnp.float32)
    return y.astype(x.dtype).reshape(shp)


def setup_inputs(seed: int = 0) -> dict:
    key = jax.random.key(seed)
    ks = jax.random.split(key, 24)

    def nrm(k, shape, scale):
        return jax.random.normal(k, shape, jnp.float32) * scale

    D, L = D_MODEL, DEPTH
    return {
        "x_prompt": nrm(ks[0], (BATCH, SEQ, D), 1.0),
        "x_sample": nrm(ks[1], (DEC_BATCH, DEC_SEQ, D), 1.0),
        "cache_kv_w128": nrm(ks[2], (L, DEC_BATCH, min(DILATED_GROUPS[0][0], PAST_LEN), 2, HEADS_PER_GROUP, HEAD_DIM), 1.0),
        "cache_kv_w512": nrm(ks[3], (L, DEC_BATCH, min(DILATED_GROUPS[1][0], PAST_LEN), 2, HEADS_PER_GROUP, HEAD_DIM), 1.0),
        "cache_kv_w2048": nrm(ks[4], (L, DEC_BATCH, min(DILATED_GROUPS[2][0], PAST_LEN), 2, HEADS_PER_GROUP, HEAD_DIM), 1.0),
        "state_conv": nrm(ks[5], (L, DEC_BATCH, CONV_WIDTH - 1, CONV_CH), 0.5),
        "norm_mix": 1.0 + nrm(ks[6], (L, D), 0.05),
        "norm_ffn": 1.0 + nrm(ks[7], (L, D), 0.05),
        "norm_final": 1.0 + nrm(ks[8], (D,), 0.05),
        "w_in": nrm(ks[9], (L, D, IN_W), D ** -0.5),
        "w_dw": nrm(ks[10], (L, CONV_WIDTH, CONV_CH), CONV_WIDTH ** -0.5),
        "b_dw": nrm(ks[11], (L, CONV_CH), 0.02),
        "conv_norm_g": 1.0 + nrm(ks[12], (L, CONV_CH), 0.05),
        "conv_norm_b": nrm(ks[13], (L, CONV_CH), 0.02),
        "w_branch": nrm(ks[14], (L, ATTN_OUT + CONV_CH, D), (ATTN_OUT + CONV_CH) ** -0.5),
        "w_out": nrm(ks[15], (L, D, D), D ** -0.5),
        "w_route_group": nrm(ks[16], (L, D, N_COARSE), D ** -0.5),
        "b_route_group": nrm(ks[17], (L, N_COARSE), 0.01),
        "w_route_expert": nrm(ks[18], (L, D, N_EXPERTS), D ** -0.5),
        "b_route_expert": nrm(ks[19], (L, N_EXPERTS), 0.01),
        "w_exp_gate": nrm(ks[20], (L, N_EXPERTS, D, D_EXPERT), D ** -0.5),
        "w_exp_up": nrm(ks[21], (L, N_EXPERTS, D, D_EXPERT), D ** -0.5),
        "w_exp_down": nrm(ks[22], (L, N_EXPERTS, D_EXPERT, D), D_EXPERT ** -0.5),
    }


def reference(x_prompt, x_sample, cache_kv_w128, cache_kv_w512, cache_kv_w2048, state_conv,
              norm_mix, norm_ffn, norm_final, w_in, w_dw, b_dw, conv_norm_g, conv_norm_b,
              w_branch, w_out, w_route_group, b_route_group, w_route_expert, b_route_expert,
              w_exp_gate, w_exp_up, w_exp_down):
    xp, xs = x_prompt, x_sample
    p_st = [[], [], [], []]
    s_st = [[], [], [], []]
    for l in range(DEPTH):
        mw = (w_in[l], w_dw[l], b_dw[l], conv_norm_g[l], conv_norm_b[l], w_branch[l], w_out[l])
        fw = (w_route_group[l], b_route_group[l], w_route_expert[l], b_route_expert[l],
              w_exp_gate[l], w_exp_up[l], w_exp_down[l])
        h, kvs, cst = mixer_prompt(rmsnorm(xp, norm_mix[l]), *mw)
        xp = xp + h
        xp = xp + hier_moe(rmsnorm(xp, norm_ffn[l]), *fw)
        for i in range(N_DGROUPS):
            p_st[i].append(kvs[i])
        p_st[3].append(cst)
        h, kvs, cst = mixer_sample(rmsnorm(xs, norm_mix[l]),
                                   (cache_kv_w128[l], cache_kv_w512[l], cache_kv_w2048[l]),
                                   state_conv[l], *mw)
        xs = xs + h
        xs = xs + hier_moe(rmsnorm(xs, norm_ffn[l]), *fw)
        for i in range(N_DGROUPS):
            s_st[i].append(kvs[i])
        s_st[3].append(cst)
    y_prompt = rmsnorm(xp, norm_final)
    y_sample = rmsnorm(xs, norm_final)
    kv128_p = jnp.stack(p_st[0], axis=0)
    kv512_p = jnp.stack(p_st[1], axis=0)
    kv2048_p = jnp.stack(p_st[2], axis=0)
    conv_p = jnp.stack(p_st[3], axis=0)
    kv128_s = jnp.stack(s_st[0], axis=0)
    kv512_s = jnp.stack(s_st[1], axis=0)
    kv2048_s = jnp.stack(s_st[2], axis=0)
    conv_s = jnp.stack(s_st[3], axis=0)
    return (y_prompt, y_sample, kv128_p, kv512_p, kv2048_p, conv_p, kv128_s, kv512_s, kv2048_s, conv_s)
```

```python
import functools

import jax
import jax.numpy as jnp
from jax import lax
from jax.experimental import pallas as pl
from jax.experimental.pallas import tpu as pltpu

D_MODEL = 1024
HEAD_DIM = 128
HEADS_PER_GROUP = 4
GROUP_W = HEADS_PER_GROUP * HEAD_DIM
DILATED_GROUPS = ((128, 1), (512, 4), (2048, 16))
N_GROUPS = len(DILATED_GROUPS)
ATTN_W = N_GROUPS * GROUP_W
SUB_BLOCK = 128
CONV_WIDTH = 31
CONV_HIST = CONV_WIDTH - 1
N_COARSE = 4
EXPERTS_PER_GROUP = 8
N_EXPERTS = N_COARSE * EXPERTS_PER_GROUP
D_EXPERT = 256
EPS = 1e-6
NEG = -1e30
LANES = 128
HALO_ROWS = 32
V7X_VMEM_BYTES = 64 * 1024 * 1024

F32 = jnp.float32
BF16 = jnp.bfloat16


def _vmem_limit(block_bytes, scratch_bytes=0):
    need = 2 * block_bytes + scratch_bytes
    return int(min(need + 16 * 1024 * 1024, V7X_VMEM_BYTES - 8 * 1024 * 1024))


def _nbytes(shape, dtype):
    n = 1
    for s in shape:
        n *= s
    return n * jnp.dtype(dtype).itemsize


def _rms_normed(x, g):
    r = lax.rsqrt(jnp.mean(x * x, axis=-1, keepdims=True) + EPS)
    return (x * r) * g


def _sigmoid(x):
    return 1.0 / (1.0 + jnp.exp(-x))


def _round_bf16(x):
    return x.astype(BF16).astype(F32)


PROJ_CHUNK = 512


def _qkv_kernel(x_ref, g_ref, w_ref, *out_refs, emit_bf16):
    xb = _rms_normed(x_ref[...], g_ref[...]).astype(BF16)
    for c in range(0, 3 * ATTN_W, PROJ_CHUNK):
        acc = jnp.dot(xb, w_ref[:, c:c + PROJ_CHUNK], preferred_element_type=F32)
        if emit_bf16:
            qkv_ref, kvf_ref = out_refs
            qkv_ref[:, c:c + PROJ_CHUNK] = acc.astype(BF16)
            if c >= ATTN_W:
                kvf_ref[:, c - ATTN_W:c - ATTN_W + PROJ_CHUNK] = acc
        else:
            (qkvf_ref,) = out_refs
            qkvf_ref[:, c:c + PROJ_CHUNK] = acc


def _glu_kernel(x_ref, g_ref, w_ref, u_ref):
    xb = _rms_normed(x_ref[...], g_ref[...]).astype(BF16)
    for c in range(0, D_MODEL, PROJ_CHUNK):
        a = jnp.dot(xb, w_ref[:, c:c + PROJ_CHUNK], preferred_element_type=F32)
        b = jnp.dot(xb, w_ref[:, D_MODEL + c:D_MODEL + c + PROJ_CHUNK], preferred_element_type=F32)
        u_ref[:, c:c + PROJ_CHUNK] = a * _sigmoid(b)


def _gate_kernel(x_ref, g_ref, w_ref, sg_ref):
    xb = _rms_normed(x_ref[...], g_ref[...]).astype(BF16)
    for c in range(0, 2 * D_MODEL, PROJ_CHUNK):
        acc = jnp.dot(xb, w_ref[:, c:c + PROJ_CHUNK], preferred_element_type=F32)
        sg_ref[:, c:c + PROJ_CHUNK] = _sigmoid(acc)


def _row_call(kernel, x, consts, out_shapes, tm, name):
    n = x.shape[0]
    in_specs = [pl.BlockSpec((tm, x.shape[1]), lambda i: (i, 0))]
    in_specs += [pl.BlockSpec(c.shape, lambda i, nd=c.ndim: (0,) * nd) for c in consts]
    out_specs = [pl.BlockSpec((tm, s.shape[1]), lambda i: (i, 0)) for s in out_shapes]
    block_bytes = _nbytes((tm, x.shape[1]), x.dtype) + sum(_nbytes(c.shape, c.dtype) for c in consts)
    block_bytes += sum(_nbytes((tm, s.shape[1]), s.dtype) for s in out_shapes)
    return pl.pallas_call(
        kernel,
        grid=(n // tm,),
        in_specs=in_specs,
        out_specs=out_specs,
        out_shape=out_shapes,
        compiler_params=pltpu.CompilerParams(
            dimension_semantics=("parallel",), vmem_limit_bytes=_vmem_limit(block_bytes)),
        name=name,
    )(x, *consts)


def _in_projection(x, g_mix, w_qkv, w_glu, w_gate, *, tm, emit_bf16, tag):
    n = x.shape[0]
    if emit_bf16:
        qkv_shapes = [jax.ShapeDtypeStruct((n, 3 * ATTN_W), BF16), jax.ShapeDtypeStruct((n, 2 * ATTN_W), F32)]
    else:
        qkv_shapes = [jax.ShapeDtypeStruct((n, 3 * ATTN_W), F32)]
    qkv_out = _row_call(functools.partial(_qkv_kernel, emit_bf16=emit_bf16), x, (g_mix, w_qkv), qkv_shapes,
                        tm, "proj_qkv_" + tag)
    (u,) = _row_call(_glu_kernel, x, (g_mix, w_glu), [jax.ShapeDtypeStruct((n, D_MODEL), F32)], tm,
                     "proj_glu_" + tag)
    (sg,) = _row_call(_gate_kernel, x, (g_mix, w_gate), [jax.ShapeDtypeStruct((n, 2 * D_MODEL), F32)], tm,
                      "proj_gate_" + tag)
    return qkv_out, u, sg


def _prompt_attn_kernel(q_ref, kp_ref, kc_ref, vp_ref, vc_ref, o_ref, lse_ref):
    j = pl.program_id(2)
    row = lax.broadcasted_iota(jnp.int32, (SUB_BLOCK, SUB_BLOCK), 0)
    col = lax.broadcasted_iota(jnp.int32, (SUB_BLOCK, SUB_BLOCK), 1)
    cur_ok = col <= row
    prev_ok = (col >= row) & (j > 0)
    scale = HEAD_DIM ** -0.5
    nt = (((1,), (1,)), ((), ()))
    for h in range(HEADS_PER_GROUP):
        sl = slice(h * HEAD_DIM, (h + 1) * HEAD_DIM)
        q = q_ref[:, sl]
        sp = lax.dot_general(q, kp_ref[:, sl], nt, preferred_element_type=F32) * scale
        sc = lax.dot_general(q, kc_ref[:, sl], nt, preferred_element_type=F32) * scale
        sp = jnp.where(prev_ok, sp, NEG)
        sc = jnp.where(cur_ok, sc, NEG)
        m = jnp.maximum(jnp.max(sp, axis=-1, keepdims=True), jnp.max(sc, axis=-1, keepdims=True))
        pp = jnp.exp(sp - m)
        pc = jnp.exp(sc - m)
        den = jnp.sum(pp, axis=-1, keepdims=True) + jnp.sum(pc, axis=-1, keepdims=True)
        o = jnp.dot((pp / den).astype(BF16), vp_ref[:, sl], preferred_element_type=F32)
        o += jnp.dot((pc / den).astype(BF16), vc_ref[:, sl], preferred_element_type=F32)
        o_ref[:, sl] = o
        lse_ref[:, sl] = jnp.broadcast_to(m + jnp.log(den), (SUB_BLOCK, HEAD_DIM))


def _prompt_attention(qkv, batch, seq, g, dil):
    sub_len = seq // dil
    cols = 3 * ATTN_W // GROUP_W
    view = qkv.reshape(batch, sub_len, dil * 3 * ATTN_W)
    blk = (None, SUB_BLOCK, GROUP_W)

    def spec(col_off, prev):
        if prev:
            return pl.BlockSpec(blk, lambda b, r, j: (b, jnp.maximum(j - 1, 0), r * cols + col_off))
        return pl.BlockSpec(blk, lambda b, r, j: (b, j, r * cols + col_off))

    out_spec = pl.BlockSpec(blk, lambda b, r, j: (b, j, r))
    out_shape = jax.ShapeDtypeStruct((batch, sub_len, dil * GROUP_W), F32)
    o, lse = pl.pallas_call(
        _prompt_attn_kernel,
        grid=(batch, dil, sub_len // SUB_BLOCK),
        in_specs=[spec(g, False), spec(N_GROUPS + g, True), spec(N_GROUPS + g, False),
                  spec(2 * N_GROUPS + g, True), spec(2 * N_GROUPS + g, False)],
        out_specs=[out_spec, out_spec],
        out_shape=[out_shape, out_shape],
        compiler_params=pltpu.CompilerParams(dimension_semantics=("parallel", "parallel", "arbitrary")),
        name=f"prompt_attn_g{g}",
    )(view, view, view, view, view)
    return o.reshape(batch * seq, GROUP_W), lse.reshape(batch * seq, GROUP_W)


CONV_ROWS = 32


def _prompt_conv_kernel(u_ref, halo_ref, w_ref, b_ref, c_ref, hist_ref, *, tiles_per_seq, tm):
    first = (pl.program_id(0) % tiles_per_seq) == 0
    hist_ref[0:HALO_ROWS, :] = jnp.where(first, 0.0, _round_bf16(halo_ref[...]))
    hist_ref[HALO_ROWS:, :] = _round_bf16(u_ref[...])
    base = HALO_ROWS - CONV_HIST
    for r0 in range(0, tm, CONV_ROWS):
        acc = jnp.zeros((CONV_ROWS, D_MODEL), F32)
        for j in range(CONV_WIDTH):
            acc = acc + _round_bf16(w_ref[j:j + 1, :]) * hist_ref[pl.ds(r0 + base + j, CONV_ROWS), :]
        c_ref[r0:r0 + CONV_ROWS, :] = acc + b_ref[...]


def _prompt_conv(u, w_dw, b_dw, seq, tm):
    n = u.shape[0]
    halo_per_tile = tm // HALO_ROWS
    kernel = functools.partial(_prompt_conv_kernel, tiles_per_seq=seq // tm, tm=tm)
    return pl.pallas_call(
        kernel,
        grid=(n // tm,),
        in_specs=[pl.BlockSpec((tm, D_MODEL), lambda i: (i, 0)),
                  pl.BlockSpec((HALO_ROWS, D_MODEL), lambda i: (jnp.maximum(i * halo_per_tile - 1, 0), 0)),
                  pl.BlockSpec(w_dw.shape, lambda i: (0, 0)),
                  pl.BlockSpec(b_dw.shape, lambda i: (0, 0))],
        out_specs=pl.BlockSpec((tm, D_MODEL), lambda i: (i, 0)),
        out_shape=jax.ShapeDtypeStruct((n, D_MODEL), F32),
        scratch_shapes=[pltpu.VMEM((tm + HALO_ROWS, D_MODEL), F32)],
        compiler_params=pltpu.CompilerParams(dimension_semantics=("parallel",)),
        name="prompt_conv",
    )(u, u, w_dw, b_dw)


def _sample_conv_kernel(state_ref, u_ref, w_ref, b_ref, new_state_ref, c_ref):
    st = state_ref[...]
    u = u_ref[...]
    c = jnp.sum(st * w_ref[0:CONV_HIST, :][None], axis=1)
    c_ref[...] = c + w_ref[CONV_HIST:CONV_WIDTH, :] * u + b_ref[...]
    new_state_ref[:, 0:CONV_HIST - 1, :] = state_ref[:, 1:CONV_HIST, :]
    new_state_ref[:, CONV_HIST - 1, :] = u


def _sample_conv(state, u, w_dw, b_dw, bb):
    nb = state.shape[0]
    return pl.pallas_call(
        _sample_conv_kernel,
        grid=(nb // bb,),
        in_specs=[pl.BlockSpec((bb, CONV_HIST, D_MODEL), lambda i: (i, 0, 0)),
                  pl.BlockSpec((bb, D_MODEL), lambda i: (i, 0)),
                  pl.BlockSpec(w_dw.shape, lambda i: (0, 0)),
                  pl.BlockSpec(b_dw.shape, lambda i: (0, 0))],
        out_specs=[pl.BlockSpec((bb, CONV_HIST, D_MODEL), lambda i: (i, 0, 0)),
                   pl.BlockSpec((bb, D_MODEL), lambda i: (i, 0))],
        out_shape=[jax.ShapeDtypeStruct(state.shape, F32), jax.ShapeDtypeStruct((nb, D_MODEL), F32)],
        compiler_params=pltpu.CompilerParams(dimension_semantics=("parallel",)),
        name="sample_conv",
    )(state, u, w_dw, b_dw)


CACHE_DMA_BATCH = 16


def _sample_attn_kernel(q_ref, kn_ref, vn_ref, kv_ref, cache_hbm, newkv_hbm, o_ref, lse_ref, out_hbm, sems,
                        *, window, n_seq):
    i = pl.program_id(0)

    def copies():
        out = []
        for c, b0 in enumerate(range(0, n_seq, CACHE_DMA_BATCH)):
            bs = pl.ds(b0, CACHE_DMA_BATCH)
            out.append(pltpu.make_async_copy(cache_hbm.at[0, bs, pl.ds(1, window - 1)],
                                             out_hbm.at[0, bs, pl.ds(0, window - 1)], sems.at[0, c]))
            out.append(pltpu.make_async_copy(newkv_hbm.at[bs], out_hbm.at[0, bs, pl.ds(window - 1, 1)],
                                             sems.at[1, c]))
        return out

    @pl.when(i == 0)
    def _():
        for cp in copies():
            cp.start()

    scale = HEAD_DIM ** -0.5
    q = _round_bf16(q_ref[...])
    k = _round_bf16(kv_ref[:, :, 0])
    v = _round_bf16(kv_ref[:, :, 1])
    s = jnp.sum(k * q[:, None], axis=-1, keepdims=True) * scale
    sn = jnp.sum(_round_bf16(kn_ref[...]) * q, axis=-1, keepdims=True) * scale
    m = jnp.maximum(jnp.max(s, axis=1), sn)
    p = jnp.exp(s - m[:, None])
    pn = jnp.exp(sn - m)
    den = jnp.sum(p, axis=1) + pn
    o = (jnp.sum(_round_bf16(p / den[:, None]) * v, axis=1)
         + _round_bf16(pn / den) * _round_bf16(vn_ref[...]))
    o_ref[...] = o
    lse_ref[...] = jnp.broadcast_to(m + jnp.log(den), o.shape)

    @pl.when(i == pl.num_programs(0) - 1)
    def _():
        for cp in copies():
            cp.wait()


def _sample_attention(qkvf, cache, g, window, dil, bb):
    n_seq = qkvf.shape[0]
    hshape = (n_seq, HEADS_PER_GROUP, HEAD_DIM)
    q = qkvf[:, g * GROUP_W:(g + 1) * GROUP_W].reshape(hshape)
    kn = qkvf[:, ATTN_W + g * GROUP_W:ATTN_W + (g + 1) * GROUP_W].reshape(hshape)
    vn = qkvf[:, 2 * ATTN_W + g * GROUP_W:2 * ATTN_W + (g + 1) * GROUP_W].reshape(hshape)
    newkv = jnp.stack([kn, vn], axis=1)[:, None]
    strided = cache.reshape(n_seq, window // dil, dil, 2, HEADS_PER_GROUP, HEAD_DIM)
    n_chunks = n_seq // CACHE_DMA_BATCH
    head_spec = pl.BlockSpec((bb, HEADS_PER_GROUP, HEAD_DIM), lambda i: (i, 0, 0))
    kernel = functools.partial(_sample_attn_kernel, window=window, n_seq=n_seq)
    o, lse, new_cache = pl.pallas_call(
        kernel,
        grid=(n_seq // bb,),
        in_specs=[head_spec, head_spec, head_spec,
                  pl.BlockSpec((bb, window // dil, None, 2, HEADS_PER_GROUP, HEAD_DIM),
                               lambda i: (i, 0, 0, 0, 0, 0)),
                  pl.BlockSpec(memory_space=pl.ANY),
                  pl.BlockSpec(memory_space=pl.ANY)],
        out_specs=[head_spec, head_spec, pl.BlockSpec(memory_space=pl.ANY)],
        out_shape=[jax.ShapeDtypeStruct(hshape, F32), jax.ShapeDtypeStruct(hshape, F32),
                   jax.ShapeDtypeStruct(cache.shape, cache.dtype)],
        scratch_shapes=[pltpu.SemaphoreType.DMA((2, n_chunks))],
        compiler_params=pltpu.CompilerParams(dimension_semantics=("arbitrary",)),
        name=f"sample_attn_g{g}",
    )(q, kn, vn, strided, cache, newkv)
    return o.reshape(n_seq, GROUP_W), lse.reshape(n_seq, GROUP_W), new_cache


def _merge_kernel(x_ref, o0_ref, o1_ref, o2_ref, l0_ref, l1_ref, l2_ref, c_ref, sg_ref, cng_ref, cnb_ref,
                  wb_ref, wo_ref, gf_ref, wr_ref, br_ref, x1_ref, t_ref, cw_ref):
    l0, l1, l2 = l0_ref[...], l1_ref[...], l2_ref[...]
    lm = jnp.maximum(jnp.maximum(l0, l1), l2)
    e0, e1, e2 = jnp.exp(l0 - lm), jnp.exp(l1 - lm), jnp.exp(l2 - lm)
    es = e0 + e1 + e2
    o_attn = (e0 / es) * o0_ref[...] + (e1 / es) * o1_ref[...] + (e2 / es) * o2_ref[...]

    cf = c_ref[...]
    mu = jnp.mean(cf, axis=-1, keepdims=True)
    var = jnp.mean(jnp.square(cf - mu), axis=-1, keepdims=True)
    cn = ((cf - mu) * lax.rsqrt(var + EPS)) * cng_ref[...] + cnb_ref[...]
    c = cn * _sigmoid(cn)

    ya = jnp.dot(o_attn.astype(BF16), wb_ref[0:GROUP_W, :], preferred_element_type=F32)
    yb = jnp.dot(c.astype(BF16), wb_ref[GROUP_W:, :], preferred_element_type=F32)
    mix = sg_ref[:, 0:D_MODEL] * ya + sg_ref[:, D_MODEL:] * yb
    x1 = x_ref[...] + jnp.dot(mix.astype(BF16), wo_ref[...], preferred_element_type=F32)
    x1_ref[...] = x1

    t = _rms_normed(x1, gf_ref[...]).astype(BF16)
    t_ref[...] = t
    logits = jnp.dot(t, wr_ref[...], preferred_element_type=F32) + br_ref[...]
    lane = lax.broadcasted_iota(jnp.int32, logits.shape, 1)
    lane_f = lane.astype(F32)
    big = float(LANES)
    is_group = (lane >= N_EXPERTS) & (lane < N_EXPERTS + N_COARSE)
    gl = jnp.where(is_group, logits, -jnp.inf)
    gmax = jnp.max(gl, axis=-1, keepdims=True)
    gsel = jnp.min(jnp.where(gl == gmax, lane_f, big), axis=-1, keepdims=True) - float(N_EXPERTS)
    gw = 1.0 / jnp.sum(jnp.where(is_group, jnp.exp(gl - gmax), 0.0), axis=-1, keepdims=True)
    in_group = (lane < N_EXPERTS) & ((lane // EXPERTS_PER_GROUP).astype(F32) == gsel)
    el = jnp.where(in_group, logits, -jnp.inf)
    v1 = jnp.max(el, axis=-1, keepdims=True)
    i1 = jnp.min(jnp.where(el == v1, lane_f, big), axis=-1, keepdims=True)
    el2 = jnp.where(lane_f == i1, -jnp.inf, el)
    v2 = jnp.max(el2, axis=-1, keepdims=True)
    i2 = jnp.min(jnp.where(el2 == v2, lane_f, big), axis=-1, keepdims=True)
    e2nd = jnp.exp(v2 - v1)
    tden = 1.0 + e2nd
    cw_ref[...] = jnp.where(lane_f == i1, (1.0 / tden) * gw, jnp.where(lane_f == i2, (e2nd / tden) * gw, 0.0))


def _merge(x, outs, lses, c_pre, sg, consts, tm, tag):
    n = x.shape[0]
    rows = [x, *outs, *lses, c_pre, sg]
    in_specs = [pl.BlockSpec((tm, a.shape[1]), lambda i: (i, 0)) for a in rows]
    in_specs += [pl.BlockSpec(c.shape, lambda i: (0, 0)) for c in consts]
    out_shapes = [jax.ShapeDtypeStruct((n, D_MODEL), F32), jax.ShapeDtypeStruct((n, D_MODEL), BF16),
                  jax.ShapeDtypeStruct((n, LANES), F32)]
    out_specs = [pl.BlockSpec((tm, s.shape[1]), lambda i: (i, 0)) for s in out_shapes]
    block_bytes = sum(_nbytes((tm, a.shape[1]), a.dtype) for a in rows + out_shapes)
    block_bytes += sum(_nbytes(c.shape, c.dtype) for c in consts)
    return pl.pallas_call(
        _merge_kernel,
        grid=(n // tm,),
        in_specs=in_specs,
        out_specs=out_specs,
        out_shape=out_shapes,
        compiler_params=pltpu.CompilerParams(
            dimension_semantics=("parallel",), vmem_limit_bytes=_vmem_limit(block_bytes)),
        name="merge_" + tag,
    )(*rows, *consts)


def _moe_kernel(x1_ref, t_ref, cw_ref, wg_ref, wu_ref, wd_ref, gn_ref, y_ref, acc_ref):
    e = pl.program_id(1)

    @pl.when(e == 0)
    def _():
        acc_ref[...] = jnp.zeros_like(acc_ref)

    t = t_ref[...]
    gate = jnp.dot(t, wg_ref[...], preferred_element_type=F32)
    up = jnp.dot(t, wu_ref[...], preferred_element_type=F32)
    lane = lax.broadcasted_iota(jnp.int32, cw_ref.shape, 1)
    w = jnp.sum(jnp.where(lane == e, cw_ref[...], 0.0), axis=-1, keepdims=True)
    h = ((gate * _sigmoid(gate)) * up) * w
    acc_ref[...] += jnp.dot(h.astype(BF16), wd_ref[...], preferred_element_type=F32)

    @pl.when(e == pl.num_programs(1) - 1)
    def _():
        y_ref[...] = _rms_normed(x1_ref[...] + acc_ref[...], gn_ref[...])


def _moe_final(x1, t, cw, w_gate, w_up, w_down, g_final, tm, tag):
    n = x1.shape[0]
    row = lambda w: pl.BlockSpec((tm, w), lambda i, e: (i, 0))
    block_bytes = (_nbytes((tm, D_MODEL), F32) * 2 + _nbytes((tm, D_MODEL), BF16) + _nbytes((tm, LANES), F32)
                   + 3 * _nbytes((D_MODEL, D_EXPERT), BF16))
    return pl.pallas_call(
        _moe_kernel,
        grid=(n // tm, N_EXPERTS),
        in_specs=[row(D_MODEL), row(D_MODEL), row(LANES),
                  pl.BlockSpec((None, D_MODEL, D_EXPERT), lambda i, e: (e, 0, 0)),
                  pl.BlockSpec((None, D_MODEL, D_EXPERT), lambda i, e: (e, 0, 0)),
                  pl.BlockSpec((None, D_EXPERT, D_MODEL), lambda i, e: (e, 0, 0)),
                  pl.BlockSpec(g_final.shape, lambda i, e: (0, 0))],
        out_specs=row(D_MODEL),
        out_shape=jax.ShapeDtypeStruct((n, D_MODEL), F32),
        scratch_shapes=[pltpu.VMEM((tm, D_MODEL), F32)],
        compiler_params=pltpu.CompilerParams(
            dimension_semantics=("parallel", "arbitrary"),
            vmem_limit_bytes=_vmem_limit(block_bytes, _nbytes((tm, D_MODEL), F32))),
        name="moe_" + tag,
    )(x1, t, cw, w_gate, w_up, w_down, g_final)


def kernel(x_prompt, x_sample, cache_kv_w128, cache_kv_w512, cache_kv_w2048, state_conv, norm_mix, norm_ffn,
           norm_final, w_in, w_dw, b_dw, conv_norm_g, conv_norm_b, w_branch, w_out, w_route_group,
           b_route_group, w_route_expert, b_route_expert, w_exp_gate, w_exp_up, w_exp_down):
    depth = w_in.shape[0]
    assert depth == 1, "single-layer step"
    batch, seq, _ = x_prompt.shape
    n_seq, dec_seq, _ = x_sample.shape
    assert dec_seq == 1
    n_prompt = batch * seq

    w_in_b = w_in[0].astype(BF16)
    w_qkv = w_in_b[:, :3 * ATTN_W]
    w_glu = w_in_b[:, 3 * ATTN_W:3 * ATTN_W + 2 * D_MODEL]
    w_gate = w_in_b[:, 3 * ATTN_W + 2 * D_MODEL:]
    g_mix = norm_mix[0][None]
    g_ffn = norm_ffn[0][None]
    g_final = norm_final[None]
    w_dw0, b_dw0 = w_dw[0], b_dw[0][None]
    w_router = jnp.concatenate([w_route_expert[0], w_route_group[0]], axis=1)
    w_router = jnp.pad(w_router, ((0, 0), (0, LANES - w_router.shape[1])))
    w_router = w_router.astype(BF16)
    b_router = jnp.concatenate([b_route_expert[0], b_route_group[0]])
    b_router = jnp.pad(b_router, (0, LANES - b_router.shape[0]))[None]
    merge_consts = (conv_norm_g[0][None], conv_norm_b[0][None], w_branch[0].astype(BF16), w_out[0].astype(BF16),
                    g_ffn, w_router, b_router)
    moe_w = (w_exp_gate[0].astype(BF16), w_exp_up[0].astype(BF16), w_exp_down[0].astype(BF16))
    caches = (cache_kv_w128, cache_kv_w512, cache_kv_w2048)

    xp = x_prompt.reshape(n_prompt, D_MODEL)
    (qkv, kvf), u_p, sg_p = _in_projection(xp, g_mix, w_qkv, w_glu, w_gate, tm=512, emit_bf16=True, tag="p")
    outs, lses = [], []
    for g, (_, dil) in enumerate(DILATED_GROUPS):
        o, l = _prompt_attention(qkv, batch, seq, g, dil)
        outs.append(o)
        lses.append(l)
    c_pre = _prompt_conv(u_p, w_dw0, b_dw0, seq, tm=256)
    x1, t, cw = _merge(xp, outs, lses, c_pre, sg_p, merge_consts, tm=256, tag="p")
    y_prompt = _moe_final(x1, t, cw, *moe_w, g_final, tm=1024, tag="p").reshape(batch, seq, D_MODEL)
    kv6 = kvf.reshape(batch, seq, 2, N_GROUPS, HEADS_PER_GROUP, HEAD_DIM)
    kv_p = [kv6[:, seq - min(win, seq):, :, g][None] for g, (win, _) in enumerate(DILATED_GROUPS)]
    conv_p = u_p.reshape(batch, seq, D_MODEL)[:, seq - CONV_HIST:][None]

    xs = x_sample.reshape(n_seq, D_MODEL)
    (qkvf,), u_s, sg_s = _in_projection(xs, g_mix, w_qkv, w_glu, w_gate, tm=n_seq, emit_bf16=False, tag="s")
    outs, lses, kv_s = [], [], []
    for g, (win, dil) in enumerate(DILATED_GROUPS):
        o, l, new_cache = _sample_attention(qkvf, caches[g], g, win, dil, bb=8)
        outs.append(o)
        lses.append(l)
        kv_s.append(new_cache)
    conv_s, c_pre_s = _sample_conv(state_conv[0], u_s, w_dw0, b_dw0, bb=32)
    x1s, ts, cws = _merge(xs, outs, lses, c_pre_s, sg_s, merge_consts, tm=n_seq, tag="s")
    y_sample = _moe_final(x1s, ts, cws, *moe_w, g_final, tm=n_seq, tag="s").reshape(n_seq, 1, D_MODEL)

    return (y_prompt, y_sample, kv_p[0], kv_p[1], kv_p[2], conv_p, kv_s[0], kv_s[1], kv_s[2], conv_s[None])
```

```python
import functools

import jax
import jax.numpy as jnp
from jax import lax
from jax.experimental import pallas as pl
from jax.experimental.pallas import tpu as pltpu

D_MODEL = 1024
HEAD_DIM = 128
HEADS_PER_GROUP = 4
GROUP_W = HEADS_PER_GROUP * HEAD_DIM
DILATED_GROUPS = ((128, 1), (512, 4), (2048, 16))
N_GROUPS = len(DILATED_GROUPS)
ATTN_W = N_GROUPS * GROUP_W
QKV_W = 3 * GROUP_W
SUB_BLOCK = 128
CONV_WIDTH = 31
CONV_HIST = CONV_WIDTH - 1
N_COARSE = 4
EXPERTS_PER_GROUP = 8
N_EXPERTS = N_COARSE * EXPERTS_PER_GROUP
D_EXPERT = 256
EPS = 1e-6
NEG = -1e30
LANES = 128
SUBLANES = 8
HALO_ROWS = 32
V7X_VMEM_BYTES = 64 * 1024 * 1024

PROMPT_PROJ_ROWS = 512
PROMPT_ATTN_ROWS = 512
PROMPT_CONV_ROWS = 256
PROMPT_MERGE_ROWS = 256
PROMPT_MOE_ROWS = 1024
SAMPLE_ATTN_SEQS = 8
SAMPLE_CONV_SEQS = 32

F32 = jnp.float32
BF16 = jnp.bfloat16


def _vmem_limit(block_bytes, scratch_bytes=0):
    need = 2 * block_bytes + scratch_bytes
    return int(min(need + 16 * 1024 * 1024, V7X_VMEM_BYTES - 8 * 1024 * 1024))


def _nbytes(shape, dtype):
    n = 1
    for s in shape:
        n *= s
    return n * jnp.dtype(dtype).itemsize


def _rms_normed(x, g):
    r = lax.rsqrt(jnp.mean(x * x, axis=-1, keepdims=True) + EPS)
    return (x * r) * g


def _sigmoid(x):
    return 1.0 / (1.0 + jnp.exp(-x))


def _round_bf16(x):
    return x.astype(BF16).astype(F32)


PROJ_CHUNK = 512


def _qkv_prompt_kernel(x_ref, g_ref, w_ref, a0_ref, a1_ref, a2_ref, t0_ref, t1_ref, t2_ref, stage_ref, *, tm):
    xb = _rms_normed(x_ref[...], g_ref[...]).astype(BF16)
    group_refs = (a0_ref, a1_ref, a2_ref)
    tail_refs = (t0_ref, t1_ref, t2_ref)
    for part in range(3):
        for g, (win, dil) in enumerate(DILATED_GROUPS):
            c0 = (part * N_GROUPS + g) * GROUP_W
            acc = jnp.dot(xb, w_ref[:, c0:c0 + GROUP_W], preferred_element_type=F32)
            cols = slice(part * GROUP_W, (part + 1) * GROUP_W)
            if dil == 1:
                group_refs[g][0, :, cols] = acc.astype(BF16)
            else:
                for h in range(HEADS_PER_GROUP):
                    stage_ref[h] = acc[:, h * HEAD_DIM:(h + 1) * HEAD_DIM]
                for r in range(dil):
                    for h in range(HEADS_PER_GROUP):
                        hc = slice(cols.start + h * HEAD_DIM, cols.start + (h + 1) * HEAD_DIM)
                        group_refs[g][r, :, hc] = stage_ref[h, pl.ds(r, tm // dil, stride=dil), :].astype(BF16)
            if part >= 1:
                keep = min(win, tm)
                tcols = slice((part - 1) * GROUP_W, part * GROUP_W)
                tail_refs[g][:, tcols] = acc[tm - keep:, :]


def _qkv_prompt(x, g_mix, w_qkv, batch, seq):
    tm = PROMPT_PROJ_ROWS
    tiles = seq // tm
    assert all(win % tm == 0 or tm % win == 0 for win, _ in DILATED_GROUPS)
    group_shapes = [jax.ShapeDtypeStruct((batch, dil, seq // dil, QKV_W), BF16) for _, dil in DILATED_GROUPS]
    group_specs = [pl.BlockSpec((None, dil, tm // dil, QKV_W), lambda b, i: (b, 0, i, 0)) for _, dil in DILATED_GROUPS]
    tail_shapes, tail_specs = [], []
    for win, _ in DILATED_GROUPS:
        keep = min(win, seq)
        rows = min(keep, tm)
        first = (seq - keep) // tm
        tail_shapes.append(jax.ShapeDtypeStruct((batch, keep, 2 * GROUP_W), F32))
        tail_specs.append(pl.BlockSpec((None, rows, 2 * GROUP_W),
                                       lambda b, i, first=first: (b, jnp.maximum(i - first, 0), 0)))
    block_bytes = (_nbytes((tm, D_MODEL), F32) + _nbytes(w_qkv.shape, BF16) + 3 * _nbytes((tm, QKV_W), BF16)
                   + 3 * _nbytes((tm, 2 * GROUP_W), F32))
    return pl.pallas_call(
        functools.partial(_qkv_prompt_kernel, tm=tm),
        grid=(batch, tiles),
        in_specs=[pl.BlockSpec((tm, D_MODEL), lambda b, i: (b * tiles + i, 0)),
                  pl.BlockSpec(g_mix.shape, lambda b, i: (0, 0)),
                  pl.BlockSpec(w_qkv.shape, lambda b, i: (0, 0))],
        out_specs=group_specs + tail_specs,
        out_shape=group_shapes + tail_shapes,
        scratch_shapes=[pltpu.VMEM((HEADS_PER_GROUP, tm, HEAD_DIM), F32)],
        compiler_params=pltpu.CompilerParams(
            dimension_semantics=("arbitrary", "arbitrary"),
            vmem_limit_bytes=_vmem_limit(block_bytes, _nbytes((tm, GROUP_W), F32))),
        name="proj_qkv_p",
    )(x, g_mix, w_qkv)


def _qkv_sample_kernel(x_ref, g_ref, w_ref, qkvf_ref):
    xb = _rms_normed(x_ref[...], g_ref[...]).astype(BF16)
    for c in range(0, 3 * ATTN_W, PROJ_CHUNK):
        qkvf_ref[:, c:c + PROJ_CHUNK] = jnp.dot(xb, w_ref[:, c:c + PROJ_CHUNK], preferred_element_type=F32)


def _glu_kernel(x_ref, g_ref, w_ref, u_ref):
    xb = _rms_normed(x_ref[...], g_ref[...]).astype(BF16)
    for c in range(0, D_MODEL, PROJ_CHUNK):
        a = jnp.dot(xb, w_ref[:, c:c + PROJ_CHUNK], preferred_element_type=F32)
        b = jnp.dot(xb, w_ref[:, D_MODEL + c:D_MODEL + c + PROJ_CHUNK], preferred_element_type=F32)
        u_ref[:, c:c + PROJ_CHUNK] = a * _sigmoid(b)


def _gate_kernel(x_ref, g_ref, w_ref, sg_ref):
    xb = _rms_normed(x_ref[...], g_ref[...]).astype(BF16)
    for c in range(0, 2 * D_MODEL, PROJ_CHUNK):
        acc = jnp.dot(xb, w_ref[:, c:c + PROJ_CHUNK], preferred_element_type=F32)
        sg_ref[:, c:c + PROJ_CHUNK] = _sigmoid(acc)


def _row_call(kernel, x, consts, out_shapes, tm, name):
    n = x.shape[0]
    in_specs = [pl.BlockSpec((tm, x.shape[1]), lambda i: (i, 0))]
    in_specs += [pl.BlockSpec(c.shape, lambda i, nd=c.ndim: (0,) * nd) for c in consts]
    out_specs = [pl.BlockSpec((tm, s.shape[1]), lambda i: (i, 0)) for s in out_shapes]
    block_bytes = _nbytes((tm, x.shape[1]), x.dtype) + sum(_nbytes(c.shape, c.dtype) for c in consts)
    block_bytes += sum(_nbytes((tm, s.shape[1]), s.dtype) for s in out_shapes)
    return pl.pallas_call(
        kernel,
        grid=(n // tm,),
        in_specs=in_specs,
        out_specs=out_specs,
        out_shape=out_shapes,
        compiler_params=pltpu.CompilerParams(
            dimension_semantics=("parallel",), vmem_limit_bytes=_vmem_limit(block_bytes)),
        name=name,
    )(x, *consts)


def _glu_and_gates(x, g_mix, w_glu, w_gate, tm, tag):
    n = x.shape[0]
    (u,) = _row_call(_glu_kernel, x, (g_mix, w_glu), [jax.ShapeDtypeStruct((n, D_MODEL), F32)], tm,
                     "proj_glu_" + tag)
    (sg,) = _row_call(_gate_kernel, x, (g_mix, w_gate), [jax.ShapeDtypeStruct((n, 2 * D_MODEL), F32)], tm,
                      "proj_gate_" + tag)
    return u, sg


def _prompt_attn_kernel(cur_ref, prev_ref, o_ref, lse_ref, *, rows):
    j = pl.program_id(2)
    row = lax.broadcasted_iota(jnp.int32, (SUB_BLOCK, SUB_BLOCK), 0)
    col = lax.broadcasted_iota(jnp.int32, (SUB_BLOCK, SUB_BLOCK), 1)
    cur_ok = col <= row
    prev_ok = col >= row
    first_ok = prev_ok & (j > 0)
    scale = HEAD_DIM ** -0.5
    nt = (((1,), (1,)), ((), ()))
    tiles = []
    for s in range(rows // SUB_BLOCK):
        rs = slice(s * SUB_BLOCK, (s + 1) * SUB_BLOCK)
        ps = slice((s - 1) * SUB_BLOCK, s * SUB_BLOCK)
        for h in range(HEADS_PER_GROUP):
            qc = slice(h * HEAD_DIM, (h + 1) * HEAD_DIM)
            kc = slice(GROUP_W + h * HEAD_DIM, GROUP_W + (h + 1) * HEAD_DIM)
            vc = slice(2 * GROUP_W + h * HEAD_DIM, 2 * GROUP_W + (h + 1) * HEAD_DIM)
            if s == 0:
                tiles.append((rs, qc, kc, vc, prev_ref, slice(None), first_ok))
            else:
                tiles.append((rs, qc, kc, vc, cur_ref, ps, prev_ok))
    scores = []
    for rs, qc, kc, vc, pref, ps, p_ok in tiles:
        q = cur_ref[rs, qc]
        sp = lax.dot_general(q, pref[ps, kc], nt, preferred_element_type=F32) * scale
        sc = lax.dot_general(q, cur_ref[rs, kc], nt, preferred_element_type=F32) * scale
        scores.append((jnp.where(p_ok, sp, NEG), jnp.where(cur_ok, sc, NEG)))
    maxes = [jnp.maximum(jnp.max(sp, axis=-1, keepdims=True), jnp.max(sc, axis=-1, keepdims=True))
             for sp, sc in scores]
    probs = [(jnp.exp(sp - m), jnp.exp(sc - m)) for (sp, sc), m in zip(scores, maxes)]
    dens = [jnp.sum(pp, axis=-1, keepdims=True) + jnp.sum(pc, axis=-1, keepdims=True) for pp, pc in probs]
    for (rs, qc, kc, vc, pref, ps, p_ok), (pp, pc), m, den in zip(tiles, probs, maxes, dens):
        o = jnp.dot((pp / den).astype(BF16), pref[ps, vc], preferred_element_type=F32)
        o += jnp.dot((pc / den).astype(BF16), cur_ref[rs, vc], preferred_element_type=F32)
        o_ref[rs, qc] = o
        lse_ref[rs, qc] = jnp.broadcast_to(m + jnp.log(den), (SUB_BLOCK, HEAD_DIM))


def _prompt_attention(a, g):
    batch, dil, sub_len, _ = a.shape
    rows = min(PROMPT_ATTN_ROWS, sub_len)
    per_step = rows // SUB_BLOCK
    out_spec = pl.BlockSpec((None, None, rows, GROUP_W), lambda b, r, j: (b, r, j, 0))
    out_shape = jax.ShapeDtypeStruct((batch, dil, sub_len, GROUP_W), F32)
    return pl.pallas_call(
        functools.partial(_prompt_attn_kernel, rows=rows),
        grid=(batch, dil, sub_len // rows),
        in_specs=[pl.BlockSpec((None, None, rows, QKV_W), lambda b, r, j: (b, r, j, 0)),
                  pl.BlockSpec((None, None, SUB_BLOCK, QKV_W),
                               lambda b, r, j: (b, r, jnp.maximum(j * per_step - 1, 0), 0))],
        out_specs=[out_spec, out_spec],
        out_shape=[out_shape, out_shape],
        compiler_params=pltpu.CompilerParams(dimension_semantics=("parallel", "parallel", "arbitrary")),
        name=f"prompt_attn_g{g}",
    )(a, a)


CONV_CHUNK_ROWS = 64
CONV_CHUNK_COLS = 256
CONV_PAD_ROWS = 16


def _prompt_conv_kernel(u_ref, halo_ref, w_ref, b_ref, c_ref, hist_ref, part_ref, *, tiles_per_seq, tm):
    first = (pl.program_id(0) % tiles_per_seq) == 0
    hist_ref[0:HALO_ROWS, :] = jnp.where(first, 0.0, _round_bf16(halo_ref[...]))
    hist_ref[HALO_ROWS:HALO_ROWS + tm, :] = _round_bf16(u_ref[...])
    hist_ref[HALO_ROWS + tm:, :] = jnp.zeros((CONV_PAD_ROWS, D_MODEL), F32)
    lead = HALO_ROWS - CONV_HIST
    span = CONV_CHUNK_ROWS + SUBLANES
    for r0 in range(0, tm, CONV_CHUNK_ROWS):
        for c0 in range(0, D_MODEL, CONV_CHUNK_COLS):
            cs = slice(c0, c0 + CONV_CHUNK_COLS)
            for b in range(SUBLANES):
                part = jnp.zeros((span, CONV_CHUNK_COLS), F32)
                for a in range((CONV_WIDTH + lead + SUBLANES - 1) // SUBLANES):
                    j = SUBLANES * a + b - lead
                    if 0 <= j < CONV_WIDTH:
                        part = part + _round_bf16(w_ref[j:j + 1, cs]) * hist_ref[r0 + SUBLANES * a:
                                                                                  r0 + SUBLANES * a + span, cs]
                part_ref[b] = part
            acc = part_ref[0, 0:CONV_CHUNK_ROWS, :]
            for b in range(1, SUBLANES):
                acc = acc + part_ref[b, pl.ds(b, CONV_CHUNK_ROWS), :]
            c_ref[r0:r0 + CONV_CHUNK_ROWS, cs] = acc + b_ref[:, cs]


def _prompt_conv(u, w_dw, b_dw, seq):
    tm = PROMPT_CONV_ROWS
    n = u.shape[0]
    halo_per_tile = tm // HALO_ROWS
    kernel = functools.partial(_prompt_conv_kernel, tiles_per_seq=seq // tm, tm=tm)
    return pl.pallas_call(
        kernel,
        grid=(n // tm,),
        in_specs=[pl.BlockSpec((tm, D_MODEL), lambda i: (i, 0)),
                  pl.BlockSpec((HALO_ROWS, D_MODEL), lambda i: (jnp.maximum(i * halo_per_tile - 1, 0), 0)),
                  pl.BlockSpec(w_dw.shape, lambda i: (0, 0)),
                  pl.BlockSpec(b_dw.shape, lambda i: (0, 0))],
        out_specs=pl.BlockSpec((tm, D_MODEL), lambda i: (i, 0)),
        out_shape=jax.ShapeDtypeStruct((n, D_MODEL), F32),
        scratch_shapes=[pltpu.VMEM((HALO_ROWS + tm + CONV_PAD_ROWS, D_MODEL), F32),
                        pltpu.VMEM((SUBLANES, CONV_CHUNK_ROWS + SUBLANES, CONV_CHUNK_COLS), F32)],
        compiler_params=pltpu.CompilerParams(dimension_semantics=("parallel",)),
        name="prompt_conv",
    )(u, u, w_dw, b_dw)


def _sample_conv_kernel(state_ref, u_ref, w_ref, b_ref, new_state_ref, c_ref):
    st = state_ref[...]
    u = u_ref[...]
    c = jnp.sum(st * w_ref[0:CONV_HIST, :][None], axis=1)
    c_ref[...] = c + w_ref[CONV_HIST:CONV_WIDTH, :] * u + b_ref[...]
    new_state_ref[:, 0:CONV_HIST - 1, :] = state_ref[:, 1:CONV_HIST, :]
    new_state_ref[:, CONV_HIST - 1, :] = u


def _sample_conv(state, u, w_dw, b_dw):
    bb = SAMPLE_CONV_SEQS
    nb = state.shape[0]
    return pl.pallas_call(
        _sample_conv_kernel,
        grid=(nb // bb,),
        in_specs=[pl.BlockSpec((bb, CONV_HIST, D_MODEL), lambda i: (i, 0, 0)),
                  pl.BlockSpec((bb, D_MODEL), lambda i: (i, 0)),
                  pl.BlockSpec(w_dw.shape, lambda i: (0, 0)),
                  pl.BlockSpec(b_dw.shape, lambda i: (0, 0))],
        out_specs=[pl.BlockSpec((bb, CONV_HIST, D_MODEL), lambda i: (i, 0, 0)),
                   pl.BlockSpec((bb, D_MODEL), lambda i: (i, 0))],
        out_shape=[jax.ShapeDtypeStruct(state.shape, F32), jax.ShapeDtypeStruct((nb, D_MODEL), F32)],
        compiler_params=pltpu.CompilerParams(dimension_semantics=("parallel",)),
        name="sample_conv",
    )(state, u, w_dw, b_dw)


def _sample_attn_kernel(q_ref, kn_ref, vn_ref, kv_ref, o_ref, lse_ref):
    scale = HEAD_DIM ** -0.5
    q = _round_bf16(q_ref[...])
    k = _round_bf16(kv_ref[:, :, 0])
    v = _round_bf16(kv_ref[:, :, 1])
    s = jnp.sum(k * q[:, None], axis=-1, keepdims=True) * scale
    sn = jnp.sum(_round_bf16(kn_ref[...]) * q, axis=-1, keepdims=True) * scale
    m = jnp.maximum(jnp.max(s, axis=1), sn)
    p = jnp.exp(s - m[:, None])
    pn = jnp.exp(sn - m)
    den = jnp.sum(p, axis=1) + pn
    o = (jnp.sum(_round_bf16(p / den[:, None]) * v, axis=1)
         + _round_bf16(pn / den) * _round_bf16(vn_ref[...]))
    o_ref[...] = o
    lse_ref[...] = jnp.broadcast_to(m + jnp.log(den), o.shape)


def _sample_attention(q, kn, vn, cache, g, window, dil):
    bb = SAMPLE_ATTN_SEQS
    n_seq = q.shape[0]
    strided = cache.reshape(n_seq, window // dil, dil, 2, HEADS_PER_GROUP, HEAD_DIM)
    head_spec = pl.BlockSpec((bb, HEADS_PER_GROUP, HEAD_DIM), lambda i: (i, 0, 0))
    return pl.pallas_call(
        _sample_attn_kernel,
        grid=(n_seq // bb,),
        in_specs=[head_spec, head_spec, head_spec,
                  pl.BlockSpec((bb, window // dil, None, 2, HEADS_PER_GROUP, HEAD_DIM),
                               lambda i: (i, 0, 0, 0, 0, 0))],
        out_specs=[head_spec, head_spec],
        out_shape=[jax.ShapeDtypeStruct(q.shape, F32), jax.ShapeDtypeStruct(q.shape, F32)],
        compiler_params=pltpu.CompilerParams(dimension_semantics=("parallel",)),
        name=f"sample_attn_g{g}",
    )(q, kn, vn, strided)


SHIFT_CHUNK_ROWS = 256
SHIFT_READS_AHEAD = 4
SHIFT_WRITES_BEHIND = 4
SHIFT_SLOTS = SHIFT_READS_AHEAD + SHIFT_WRITES_BEHIND
HOSTED_CHUNKS_PER_STEP = 2


def _shift_geometry(n_seq, window):
    n_copy = n_seq * window - 1
    n_full = n_copy // SHIFT_CHUNK_ROWS
    return n_full, n_copy - n_full * SHIFT_CHUNK_ROWS


def _shift_in(src, buf_slot, sem, c):
    return pltpu.make_async_copy(src.at[pl.ds(c * SHIFT_CHUNK_ROWS + 1, SHIFT_CHUNK_ROWS)], buf_slot, sem)


def _shift_out(dst, buf_slot, sem, c):
    return pltpu.make_async_copy(buf_slot, dst.at[pl.ds(c * SHIFT_CHUNK_ROWS, SHIFT_CHUNK_ROWS)], sem)


def _shift_finish(src, fresh, dst, buf_slot, sem_a, sem_b, sem_fresh, *, window, n_seq):
    n_full, tail = _shift_geometry(n_seq, window)
    if tail:
        start = n_full * SHIFT_CHUNK_ROWS
        tail_in = pltpu.make_async_copy(src.at[pl.ds(start + 1, tail)], buf_slot.at[pl.ds(0, tail)], sem_a)
        tail_in.start()
        tail_in.wait()
        tail_out = pltpu.make_async_copy(buf_slot.at[pl.ds(0, tail)], dst.at[pl.ds(start, tail)], sem_b)
        tail_out.start()
        tail_out.wait()

    def fresh_copy(b):
        return pltpu.make_async_copy(fresh.at[b], dst.at[pl.ds(b * window + window - 1, 1)], sem_fresh)

    def start_fresh(b, carry):
        fresh_copy(b).start()
        return carry

    def wait_fresh(b, carry):
        fresh_copy(b).wait()
        return carry

    lax.fori_loop(0, n_seq, start_fresh, 0)
    lax.fori_loop(0, n_seq, wait_fresh, 0)


def _cache_shift_kernel(src, fresh, dst, buf, sem_in, sem_out, sem_fresh, *, window, n_seq):
    n_full, _ = _shift_geometry(n_seq, window)
    ahead, behind = SHIFT_READS_AHEAD, SHIFT_WRITES_BEHIND
    assert n_full >= SHIFT_SLOTS
    for c in range(ahead):
        _shift_in(src, buf.at[c], sem_in.at[c], c).start()

    def body(c, carry):
        slot = c % SHIFT_SLOTS
        _shift_in(src, buf.at[slot], sem_in.at[slot], c).wait()
        _shift_out(dst, buf.at[slot], sem_out.at[slot], c).start()
        reuse = (c + ahead) % SHIFT_SLOTS

        @pl.when(c >= behind)
        def _():
            _shift_out(dst, buf.at[reuse], sem_out.at[reuse], c - behind).wait()

        @pl.when(c + ahead < n_full)
        def _():
            _shift_in(src, buf.at[reuse], sem_in.at[reuse], c + ahead).start()

        return carry

    lax.fori_loop(0, n_full, body, 0)
    for c in range(n_full - behind, n_full):
        _shift_out(dst, buf.at[c % SHIFT_SLOTS], sem_out.at[c % SHIFT_SLOTS], c).wait()
    _shift_finish(src, fresh, dst, buf.at[0], sem_in.at[0], sem_out.at[0], sem_fresh.at[0],
                  window=window, n_seq=n_seq)


def _flat_cache(cache):
    _, n_seq, window = cache.shape[:3]
    return cache.reshape((n_seq * window,) + cache.shape[3:])


def _cache_shift(cache, fresh, g):
    _, n_seq, window = cache.shape[:3]
    row = cache.shape[3:]
    flat = _flat_cache(cache)
    out = pl.pallas_call(
        functools.partial(_cache_shift_kernel, window=window, n_seq=n_seq),
        in_specs=[pl.BlockSpec(memory_space=pl.ANY), pl.BlockSpec(memory_space=pl.ANY)],
        out_specs=pl.BlockSpec(memory_space=pl.ANY),
        out_shape=jax.ShapeDtypeStruct(flat.shape, flat.dtype),
        scratch_shapes=[pltpu.VMEM((SHIFT_SLOTS, SHIFT_CHUNK_ROWS) + row, cache.dtype),
                        pltpu.SemaphoreType.DMA((SHIFT_SLOTS,)), pltpu.SemaphoreType.DMA((SHIFT_SLOTS,)),
                        pltpu.SemaphoreType.DMA((1,))],
        name=f"cache_shift_g{g}",
    )(flat, fresh)
    return out.reshape(cache.shape)


def _hosted_shift_step(src, dst, buf, sem_in, sem_out, step, *, n_full):
    cur = step % 2
    prv = 1 - cur
    for q in range(HOSTED_CHUNKS_PER_STEP):
        c = step * HOSTED_CHUNKS_PER_STEP + q
        c1 = c - HOSTED_CHUNKS_PER_STEP
        c2 = c - 2 * HOSTED_CHUNKS_PER_STEP

        @pl.when((c1 >= 0) & (c1 < n_full))
        def _():
            _shift_in(src, buf.at[prv, q], sem_in.at[prv, q], c1).wait()
            _shift_out(dst, buf.at[prv, q], sem_out.at[prv, q], c1).start()

        @pl.when((c2 >= 0) & (c2 < n_full))
        def _():
            _shift_out(dst, buf.at[cur, q], sem_out.at[cur, q], c2).wait()

        @pl.when(c < n_full)
        def _():
            _shift_in(src, buf.at[cur, q], sem_in.at[cur, q], c).start()


def _hosted_shift_drain(src, fresh, dst, buf, sem_in, sem_out, sem_fresh, step, *, window, n_seq):
    n_full, _ = _shift_geometry(n_seq, window)
    cur = step % 2
    prv = 1 - cur
    for q in range(HOSTED_CHUNKS_PER_STEP):
        c = step * HOSTED_CHUNKS_PER_STEP + q
        c1 = c - HOSTED_CHUNKS_PER_STEP

        @pl.when(c < n_full)
        def _():
            _shift_in(src, buf.at[cur, q], sem_in.at[cur, q], c).wait()
            _shift_out(dst, buf.at[cur, q], sem_out.at[cur, q], c).start()

        @pl.when((c1 >= 0) & (c1 < n_full))
        def _():
            _shift_out(dst, buf.at[prv, q], sem_out.at[prv, q], c1).wait()

        @pl.when(c < n_full)
        def _():
            _shift_out(dst, buf.at[cur, q], sem_out.at[cur, q], c).wait()

    _shift_finish(src, fresh, dst, buf.at[0, 0], sem_in.at[0, 0], sem_out.at[0, 0], sem_fresh.at[0],
                  window=window, n_seq=n_seq)


def _merge_kernel(x_ref, o0_ref, o1_ref, o2_ref, l0_ref, l1_ref, l2_ref, c_ref, sg_ref, cng_ref, cnb_ref,
                  wb_ref, wo_ref, gf_ref, wr_ref, br_ref, x1_ref, t_ref, cw_ref, order_ref):
    def token_order(ref, k):
        dil = ref.shape[0]
        if dil == 1:
            return ref[0]
        for r in range(dil):
            for h in range(HEADS_PER_GROUP):
                order_ref[k, h, pl.ds(r, ref.shape[1], stride=dil), :] = ref[r, :, h * HEAD_DIM:(h + 1) * HEAD_DIM]
        return jnp.concatenate([order_ref[k, h] for h in range(HEADS_PER_GROUP)], axis=-1)

    l0, l1, l2 = token_order(l0_ref, 0), token_order(l1_ref, 1), token_order(l2_ref, 2)
    lm = jnp.maximum(jnp.maximum(l0, l1), l2)
    e0, e1, e2 = jnp.exp(l0 - lm), jnp.exp(l1 - lm), jnp.exp(l2 - lm)
    es = e0 + e1 + e2
    o_attn = (e0 / es) * token_order(o0_ref, 3)
    o_attn = o_attn + (e1 / es) * token_order(o1_ref, 4)
    o_attn = o_attn + (e2 / es) * token_order(o2_ref, 5)

    cf = c_ref[...]
    mu = jnp.mean(cf, axis=-1, keepdims=True)
    var = jnp.mean(jnp.square(cf - mu), axis=-1, keepdims=True)
    cn = ((cf - mu) * lax.rsqrt(var + EPS)) * cng_ref[...] + cnb_ref[...]
    c = cn * _sigmoid(cn)

    ya = jnp.dot(o_attn.astype(BF16), wb_ref[0:GROUP_W, :], preferred_element_type=F32)
    yb = jnp.dot(c.astype(BF16), wb_ref[GROUP_W:, :], preferred_element_type=F32)
    mix = sg_ref[:, 0:D_MODEL] * ya + sg_ref[:, D_MODEL:] * yb
    x1 = x_ref[...] + jnp.dot(mix.astype(BF16), wo_ref[...], preferred_element_type=F32)
    x1_ref[...] = x1

    t = _rms_normed(x1, gf_ref[...]).astype(BF16)
    t_ref[...] = t
    logits = jnp.dot(t, wr_ref[...], preferred_element_type=F32) + br_ref[...]
    lane = lax.broadcasted_iota(jnp.int32, logits.shape, 1)
    lane_f = lane.astype(F32)
    big = float(LANES)
    is_group = (lane >= N_EXPERTS) & (lane < N_EXPERTS + N_COARSE)
    gl = jnp.where(is_group, logits, -jnp.inf)
    gmax = jnp.max(gl, axis=-1, keepdims=True)
    gsel = jnp.min(jnp.where(gl == gmax, lane_f, big), axis=-1, keepdims=True) - float(N_EXPERTS)
    gw = 1.0 / jnp.sum(jnp.where(is_group, jnp.exp(gl - gmax), 0.0), axis=-1, keepdims=True)
    in_group = (lane < N_EXPERTS) & ((lane // EXPERTS_PER_GROUP).astype(F32) == gsel)
    el = jnp.where(in_group, logits, -jnp.inf)
    v1 = jnp.max(el, axis=-1, keepdims=True)
    i1 = jnp.min(jnp.where(el == v1, lane_f, big), axis=-1, keepdims=True)
    el2 = jnp.where(lane_f == i1, -jnp.inf, el)
    v2 = jnp.max(el2, axis=-1, keepdims=True)
    i2 = jnp.min(jnp.where(el2 == v2, lane_f, big), axis=-1, keepdims=True)
    e2nd = jnp.exp(v2 - v1)
    tden = 1.0 + e2nd
    cw_ref[...] = jnp.where(lane_f == i1, (1.0 / tden) * gw, jnp.where(lane_f == i2, (e2nd / tden) * gw, 0.0))


def _merge(x, outs, lses, c_pre, sg, consts, batch, tm, tag):
    n = x.shape[0]
    tiles = n // (batch * tm)
    rows = [x, c_pre, sg]
    row_spec = lambda a: pl.BlockSpec((tm, a.shape[1]), lambda b, i: (b * tiles + i, 0))
    grp_spec = lambda a: pl.BlockSpec((None, a.shape[1], tm // a.shape[1], GROUP_W), lambda b, i: (b, 0, i, 0))
    in_specs = ([row_spec(x)] + [grp_spec(a) for a in outs] + [grp_spec(a) for a in lses]
                + [row_spec(c_pre), row_spec(sg)] + [pl.BlockSpec(c.shape, lambda b, i: (0, 0)) for c in consts])
    out_shapes = [jax.ShapeDtypeStruct((n, D_MODEL), F32), jax.ShapeDtypeStruct((n, D_MODEL), BF16),
                  jax.ShapeDtypeStruct((n, LANES), F32)]
    out_specs = [row_spec(s) for s in out_shapes]
    block_bytes = sum(_nbytes((tm, a.shape[1]), a.dtype) for a in rows + out_shapes)
    block_bytes += 6 * _nbytes((tm, GROUP_W), F32) + sum(_nbytes(c.shape, c.dtype) for c in consts)
    return pl.pallas_call(
        _merge_kernel,
        grid=(batch, tiles),
        in_specs=in_specs,
        out_specs=out_specs,
        out_shape=out_shapes,
        scratch_shapes=[pltpu.VMEM((2 * N_GROUPS, HEADS_PER_GROUP, tm, HEAD_DIM), F32)],
        compiler_params=pltpu.CompilerParams(
            dimension_semantics=("parallel", "parallel"),
            vmem_limit_bytes=_vmem_limit(block_bytes, _nbytes((2 * N_GROUPS, tm, GROUP_W), F32))),
        name="merge_" + tag,
    )(x, *outs, *lses, c_pre, sg, *consts)


def _moe_kernel(x1_ref, t_ref, cw_ref, wg_ref, wu_ref, wd_ref, gn_ref, *rest, shift):
    if shift is None:
        y_ref, acc_ref = rest
    else:
        src, fresh, y_ref, dst, acc_ref, buf, sem_in, sem_out, sem_fresh = rest
        step = pl.program_id(0) * pl.num_programs(1) + pl.program_id(1)
        n_full, _ = _shift_geometry(shift["n_seq"], shift["window"])
        _hosted_shift_step(src, dst, buf, sem_in, sem_out, step, n_full=n_full)
    e = pl.program_id(1)

    @pl.when(e == 0)
    def _():
        acc_ref[...] = jnp.zeros_like(acc_ref)

    t = t_ref[...]
    gate = jnp.dot(t, wg_ref[...], preferred_element_type=F32)
    up = jnp.dot(t, wu_ref[...], preferred_element_type=F32)
    lane = lax.broadcasted_iota(jnp.int32, cw_ref.shape, 1)
    w = jnp.sum(jnp.where(lane == e, cw_ref[...], 0.0), axis=-1, keepdims=True)
    h = ((gate * _sigmoid(gate)) * up) * w
    acc_ref[...] += jnp.dot(h.astype(BF16), wd_ref[...], preferred_element_type=F32)

    @pl.when(e == pl.num_programs(1) - 1)
    def _():
        y_ref[...] = _rms_normed(x1_ref[...] + acc_ref[...], gn_ref[...])

    if shift is not None:
        @pl.when(step == pl.num_programs(0) * pl.num_programs(1) - 1)
        def _():
            _hosted_shift_drain(src, fresh, dst, buf, sem_in, sem_out, sem_fresh, step, **shift)


def _moe_final(x1, t, cw, w_gate, w_up, w_down, g_final, tm, tag, hosted=None):
    n = x1.shape[0]
    grid = (n // tm, N_EXPERTS)
    row = lambda w: pl.BlockSpec((tm, w), lambda i, e: (i, 0))
    in_specs = [row(D_MODEL), row(D_MODEL), row(LANES),
                pl.BlockSpec((None, D_MODEL, D_EXPERT), lambda i, e: (e, 0, 0)),
                pl.BlockSpec((None, D_MODEL, D_EXPERT), lambda i, e: (e, 0, 0)),
                pl.BlockSpec((None, D_EXPERT, D_MODEL), lambda i, e: (e, 0, 0)),
                pl.BlockSpec(g_final.shape, lambda i, e: (0, 0))]
    out_specs = [row(D_MODEL)]
    out_shapes = [jax.ShapeDtypeStruct((n, D_MODEL), F32)]
    scratch = [pltpu.VMEM((tm, D_MODEL), F32)]
    operands = [x1, t, cw, w_gate, w_up, w_down, g_final]
    scratch_bytes = _nbytes((tm, D_MODEL), F32)
    shift = None
    if hosted is not None:
        cache, fresh = hosted
        _, n_seq, window = cache.shape[:3]
        flat = _flat_cache(cache)
        n_full, _ = _shift_geometry(n_seq, window)
        assert grid[0] * grid[1] * HOSTED_CHUNKS_PER_STEP >= n_full
        shift = dict(window=window, n_seq=n_seq)
        in_specs += [pl.BlockSpec(memory_space=pl.ANY), pl.BlockSpec(memory_space=pl.ANY)]
        out_specs.append(pl.BlockSpec(memory_space=pl.ANY))
        out_shapes.append(jax.ShapeDtypeStruct(flat.shape, flat.dtype))
        buf_shape = (2, HOSTED_CHUNKS_PER_STEP, SHIFT_CHUNK_ROWS) + flat.shape[1:]
        scratch += [pltpu.VMEM(buf_shape, flat.dtype),
                    pltpu.SemaphoreType.DMA((2, HOSTED_CHUNKS_PER_STEP)),
                    pltpu.SemaphoreType.DMA((2, HOSTED_CHUNKS_PER_STEP)),
                    pltpu.SemaphoreType.DMA((1,))]
        operands += [flat, fresh]
        scratch_bytes += _nbytes(buf_shape, flat.dtype)
    block_bytes = (_nbytes((tm, D_MODEL), F32) * 2 + _nbytes((tm, D_MODEL), BF16) + _nbytes((tm, LANES), F32)
                   + 3 * _nbytes((D_MODEL, D_EXPERT), BF16))
    outs = pl.pallas_call(
        functools.partial(_moe_kernel, shift=shift),
        grid=grid,
        in_specs=in_specs,
        out_specs=out_specs,
        out_shape=out_shapes,
        scratch_shapes=scratch,
        compiler_params=pltpu.CompilerParams(
            dimension_semantics=("arbitrary", "arbitrary"),
            vmem_limit_bytes=_vmem_limit(block_bytes, scratch_bytes)),
        name="moe_" + tag,
    )(*operands)
    if hosted is None:
        return outs[0]
    return outs[0], outs[1].reshape(hosted[0].shape)


def kernel(x_prompt, x_sample, cache_kv_w128, cache_kv_w512, cache_kv_w2048, state_conv, norm_mix, norm_ffn,
           norm_final, w_in, w_dw, b_dw, conv_norm_g, conv_norm_b, w_branch, w_out, w_route_group,
           b_route_group, w_route_expert, b_route_expert, w_exp_gate, w_exp_up, w_exp_down):
    depth = w_in.shape[0]
    assert depth == 1, "single-layer step"
    batch, seq, _ = x_prompt.shape
    n_seq, dec_seq, _ = x_sample.shape
    assert dec_seq == 1
    n_prompt = batch * seq

    w_in_b = w_in[0].astype(BF16)
    w_qkv = w_in_b[:, :3 * ATTN_W]
    w_glu = w_in_b[:, 3 * ATTN_W:3 * ATTN_W + 2 * D_MODEL]
    w_gate = w_in_b[:, 3 * ATTN_W + 2 * D_MODEL:]
    g_mix = norm_mix[0][None]
    g_ffn = norm_ffn[0][None]
    g_final = norm_final[None]
    w_dw0, b_dw0 = w_dw[0], b_dw[0][None]
    w_router = jnp.concatenate([w_route_expert[0], w_route_group[0]], axis=1)
    w_router = jnp.pad(w_router, ((0, 0), (0, LANES - w_router.shape[1]))).astype(BF16)
    b_router = jnp.concatenate([b_route_expert[0], b_route_group[0]])
    b_router = jnp.pad(b_router, (0, LANES - b_router.shape[0]))[None]
    merge_consts = (conv_norm_g[0][None], conv_norm_b[0][None], w_branch[0].astype(BF16), w_out[0].astype(BF16),
                    g_ffn, w_router, b_router)
    moe_w = (w_exp_gate[0].astype(BF16), w_exp_up[0].astype(BF16), w_exp_down[0].astype(BF16))
    caches = (cache_kv_w128, cache_kv_w512, cache_kv_w2048)

    xs = x_sample.reshape(n_seq, D_MODEL)
    (qkvf,) = _row_call(_qkv_sample_kernel, xs, (g_mix, w_qkv), [jax.ShapeDtypeStruct((n_seq, 3 * ATTN_W), F32)],
                        n_seq, "proj_qkv_s")
    u_s, sg_s = _glu_and_gates(xs, g_mix, w_glu, w_gate, n_seq, "s")
    qkv5 = qkvf.reshape(n_seq, 3, N_GROUPS, HEADS_PER_GROUP, HEAD_DIM)
    outs_s, lses_s, fresh = [], [], []
    for g, (win, dil) in enumerate(DILATED_GROUPS):
        q, kn, vn = qkv5[:, 0, g], qkv5[:, 1, g], qkv5[:, 2, g]
        o, l = _sample_attention(q, kn, vn, caches[g], g, win, dil)
        outs_s.append(o.reshape(1, 1, n_seq, GROUP_W))
        lses_s.append(l.reshape(1, 1, n_seq, GROUP_W))
        fresh.append(jnp.stack([kn, vn], axis=1)[:, None])
    hosted_g = max(range(N_GROUPS), key=lambda g: DILATED_GROUPS[g][0])
    kv_s = [None if g == hosted_g else _cache_shift(caches[g], fresh[g], g) for g in range(N_GROUPS)]

    xp = x_prompt.reshape(n_prompt, D_MODEL)
    a0, a1, a2, t0, t1, t2 = _qkv_prompt(xp, g_mix, w_qkv, batch, seq)
    u_p, sg_p = _glu_and_gates(xp, g_mix, w_glu, w_gate, PROMPT_PROJ_ROWS, "p")
    outs, lses = [], []
    for g, a in enumerate((a0, a1, a2)):
        o, l = _prompt_attention(a, g)
        outs.append(o)
        lses.append(l)
    c_pre = _prompt_conv(u_p, w_dw0, b_dw0, seq)
    x1, t, cw = _merge(xp, outs, lses, c_pre, sg_p, merge_consts, batch, PROMPT_MERGE_ROWS, "p")
    y_prompt, kv_s[hosted_g] = _moe_final(x1, t, cw, *moe_w, g_final, PROMPT_MOE_ROWS, "p",
                                          hosted=(caches[hosted_g], fresh[hosted_g]))
    y_prompt = y_prompt.reshape(batch, seq, D_MODEL)
    kv_p = [tail.reshape(batch, tail.shape[1], 2, HEADS_PER_GROUP, HEAD_DIM)[None] for tail in (t0, t1, t2)]
    conv_p = u_p.reshape(batch, seq, D_MODEL)[:, seq - CONV_HIST:][None]

    conv_s, c_pre_s = _sample_conv(state_conv[0], u_s, w_dw0, b_dw0)
    x1s, ts, cws = _merge(xs, outs_s, lses_s, c_pre_s, sg_s, merge_consts, 1, n_seq, "s")
    y_sample = _moe_final(x1s, ts, cws, *moe_w, g_final, n_seq, "s").reshape(n_seq, 1, D_MODEL)

    return (y_prompt, y_sample, kv_p[0], kv_p[1], kv_p[2], conv_p, kv_s[0], kv_s[1], kv_s[2], conv_s[None])
```

```python
import functools

import jax
import jax.numpy as jnp
from jax import lax
from jax.experimental import pallas as pl
from jax.experimental.pallas import tpu as pltpu

D_MODEL = 1024
HEAD_DIM = 128
HEADS_PER_GROUP = 4
GROUP_W = HEADS_PER_GROUP * HEAD_DIM
DILATED_GROUPS = ((128, 1), (512, 4), (2048, 16))
N_GROUPS = len(DILATED_GROUPS)
ATTN_W = N_GROUPS * GROUP_W
QKV_W = 3 * GROUP_W
SUB_BLOCK = 128
CONV_WIDTH = 31
CONV_HIST = CONV_WIDTH - 1
N_COARSE = 4
EXPERTS_PER_GROUP = 8
N_EXPERTS = N_COARSE * EXPERTS_PER_GROUP
D_EXPERT = 256
EPS = 1e-6
NEG = -1e30
LANES = 128
SUBLANES = 8
HALO_ROWS = 32
V7X_VMEM_BYTES = 64 * 1024 * 1024

PROMPT_PROJ_ROWS = 512
PROMPT_ATTN_ROWS = 512
PROMPT_CONV_ROWS = 256
PROMPT_MERGE_ROWS = 256
PROMPT_MOE_ROWS = 2048
SAMPLE_ATTN_SEQS = 8
SAMPLE_CONV_SEQS = 32

F32 = jnp.float32
BF16 = jnp.bfloat16


def _vmem_limit(block_bytes, scratch_bytes=0):
    need = 2 * block_bytes + scratch_bytes
    return int(min(need + 16 * 1024 * 1024, V7X_VMEM_BYTES - 8 * 1024 * 1024))


def _nbytes(shape, dtype):
    n = 1
    for s in shape:
        n *= s
    return n * jnp.dtype(dtype).itemsize


def _rms_normed(x, g):
    r = lax.rsqrt(jnp.mean(x * x, axis=-1, keepdims=True) + EPS)
    return (x * r) * g


def _sigmoid(x):
    return 1.0 / (1.0 + jnp.exp(-x))


def _round_bf16(x):
    return x.astype(BF16).astype(F32)


PROJ_CHUNK = 512


def _qkv_prompt_kernel(x_ref, g_ref, w_ref, a0_ref, a1_ref, a2_ref, t0_ref, t1_ref, t2_ref, stage_ref, *, tm):
    xb = _rms_normed(x_ref[...], g_ref[...]).astype(BF16)
    group_refs = (a0_ref, a1_ref, a2_ref)
    tail_refs = (t0_ref, t1_ref, t2_ref)
    for part in range(3):
        for g, (win, dil) in enumerate(DILATED_GROUPS):
            c0 = (part * N_GROUPS + g) * GROUP_W
            acc = jnp.dot(xb, w_ref[:, c0:c0 + GROUP_W], preferred_element_type=F32)
            cols = slice(part * GROUP_W, (part + 1) * GROUP_W)
            if dil == 1:
                group_refs[g][0, :, cols] = acc.astype(BF16)
            else:
                for h in range(HEADS_PER_GROUP):
                    stage_ref[h] = acc[:, h * HEAD_DIM:(h + 1) * HEAD_DIM]
                for r in range(dil):
                    for h in range(HEADS_PER_GROUP):
                        hc = slice(cols.start + h * HEAD_DIM, cols.start + (h + 1) * HEAD_DIM)
                        group_refs[g][r, :, hc] = stage_ref[h, pl.ds(r, tm // dil, stride=dil), :].astype(BF16)
            if part >= 1:
                keep = min(win, tm)
                tcols = slice((part - 1) * GROUP_W, part * GROUP_W)
                tail_refs[g][:, tcols] = acc[tm - keep:, :]


def _qkv_prompt(x, g_mix, w_qkv, batch, seq):
    tm = PROMPT_PROJ_ROWS
    tiles = seq // tm
    assert all(win % tm == 0 or tm % win == 0 for win, _ in DILATED_GROUPS)
    group_shapes = [jax.ShapeDtypeStruct((batch, dil, seq // dil, QKV_W), BF16) for _, dil in DILATED_GROUPS]
    group_specs = [pl.BlockSpec((None, dil, tm // dil, QKV_W), lambda b, i: (b, 0, i, 0)) for _, dil in DILATED_GROUPS]
    tail_shapes, tail_specs = [], []
    for win, _ in DILATED_GROUPS:
        keep = min(win, seq)
        rows = min(keep, tm)
        first = (seq - keep) // tm
        tail_shapes.append(jax.ShapeDtypeStruct((batch, keep, 2 * GROUP_W), F32))
        tail_specs.append(pl.BlockSpec((None, rows, 2 * GROUP_W),
                                       lambda b, i, first=first: (b, jnp.maximum(i - first, 0), 0)))
    block_bytes = (_nbytes((tm, D_MODEL), F32) + _nbytes(w_qkv.shape, BF16) + 3 * _nbytes((tm, QKV_W), BF16)
                   + 3 * _nbytes((tm, 2 * GROUP_W), F32))
    return pl.pallas_call(
        functools.partial(_qkv_prompt_kernel, tm=tm),
        grid=(batch, tiles),
        in_specs=[pl.BlockSpec((tm, D_MODEL), lambda b, i: (b * tiles + i, 0)),
                  pl.BlockSpec(g_mix.shape, lambda b, i: (0, 0)),
                  pl.BlockSpec(w_qkv.shape, lambda b, i: (0, 0))],
        out_specs=group_specs + tail_specs,
        out_shape=group_shapes + tail_shapes,
        scratch_shapes=[pltpu.VMEM((HEADS_PER_GROUP, tm, HEAD_DIM), F32)],
        compiler_params=pltpu.CompilerParams(
            dimension_semantics=("arbitrary", "arbitrary"),
            vmem_limit_bytes=_vmem_limit(block_bytes, _nbytes((tm, GROUP_W), F32))),
        name="proj_qkv_p",
    )(x, g_mix, w_qkv)


def _qkv_sample_kernel(x_ref, g_ref, w_ref, qkvf_ref):
    xb = _rms_normed(x_ref[...], g_ref[...]).astype(BF16)
    for c in range(0, 3 * ATTN_W, PROJ_CHUNK):
        qkvf_ref[:, c:c + PROJ_CHUNK] = jnp.dot(xb, w_ref[:, c:c + PROJ_CHUNK], preferred_element_type=F32)


def _glu_kernel(x_ref, g_ref, w_ref, u_ref):
    xb = _rms_normed(x_ref[...], g_ref[...]).astype(BF16)
    for c in range(0, D_MODEL, PROJ_CHUNK):
        a = jnp.dot(xb, w_ref[:, c:c + PROJ_CHUNK], preferred_element_type=F32)
        b = jnp.dot(xb, w_ref[:, D_MODEL + c:D_MODEL + c + PROJ_CHUNK], preferred_element_type=F32)
        u_ref[:, c:c + PROJ_CHUNK] = a * _sigmoid(b)


def _gate_kernel(x_ref, g_ref, w_ref, sg_ref):
    xb = _rms_normed(x_ref[...], g_ref[...]).astype(BF16)
    for c in range(0, 2 * D_MODEL, PROJ_CHUNK):
        acc = jnp.dot(xb, w_ref[:, c:c + PROJ_CHUNK], preferred_element_type=F32)
        sg_ref[:, c:c + PROJ_CHUNK] = _sigmoid(acc)


def _row_call(kernel, x, consts, out_shapes, tm, name):
    n = x.shape[0]
    in_specs = [pl.BlockSpec((tm, x.shape[1]), lambda i: (i, 0))]
    in_specs += [pl.BlockSpec(c.shape, lambda i, nd=c.ndim: (0,) * nd) for c in consts]
    out_specs = [pl.BlockSpec((tm, s.shape[1]), lambda i: (i, 0)) for s in out_shapes]
    block_bytes = _nbytes((tm, x.shape[1]), x.dtype) + sum(_nbytes(c.shape, c.dtype) for c in consts)
    block_bytes += sum(_nbytes((tm, s.shape[1]), s.dtype) for s in out_shapes)
    return pl.pallas_call(
        kernel,
        grid=(n // tm,),
        in_specs=in_specs,
        out_specs=out_specs,
        out_shape=out_shapes,
        compiler_params=pltpu.CompilerParams(
            dimension_semantics=("parallel",), vmem_limit_bytes=_vmem_limit(block_bytes)),
        name=name,
    )(x, *consts)


def _glu_and_gates(x, g_mix, w_glu, w_gate, tm, tag):
    n = x.shape[0]
    (u,) = _row_call(_glu_kernel, x, (g_mix, w_glu), [jax.ShapeDtypeStruct((n, D_MODEL), F32)], tm,
                     "proj_glu_" + tag)
    (sg,) = _row_call(_gate_kernel, x, (g_mix, w_gate), [jax.ShapeDtypeStruct((n, 2 * D_MODEL), F32)], tm,
                      "proj_gate_" + tag)
    return u, sg


def _prompt_attn_kernel(*refs, rows, geometry):
    ins, outs, _, host = _split_hosted(refs, 2, 2, geometry is not None)
    cur_ref, prev_ref = ins
    o_ref, lse_ref = outs
    j = pl.program_id(2)
    step = (pl.program_id(0) * pl.num_programs(1) + pl.program_id(1)) * pl.num_programs(2) + j
    if host is not None:
        _hosted_begin(host, step, geometry)
    row = lax.broadcasted_iota(jnp.int32, (SUB_BLOCK, SUB_BLOCK), 0)
    col = lax.broadcasted_iota(jnp.int32, (SUB_BLOCK, SUB_BLOCK), 1)
    cur_ok = col <= row
    prev_ok = col >= row
    first_ok = prev_ok & (j > 0)
    scale = HEAD_DIM ** -0.5
    nt = (((1,), (1,)), ((), ()))
    tiles = []
    for s in range(rows // SUB_BLOCK):
        rs = slice(s * SUB_BLOCK, (s + 1) * SUB_BLOCK)
        ps = slice((s - 1) * SUB_BLOCK, s * SUB_BLOCK)
        for h in range(HEADS_PER_GROUP):
            qc = slice(h * HEAD_DIM, (h + 1) * HEAD_DIM)
            kc = slice(GROUP_W + h * HEAD_DIM, GROUP_W + (h + 1) * HEAD_DIM)
            vc = slice(2 * GROUP_W + h * HEAD_DIM, 2 * GROUP_W + (h + 1) * HEAD_DIM)
            if s == 0:
                tiles.append((rs, qc, kc, vc, prev_ref, slice(None), first_ok))
            else:
                tiles.append((rs, qc, kc, vc, cur_ref, ps, prev_ok))
    scores = []
    for rs, qc, kc, vc, pref, ps, p_ok in tiles:
        q = cur_ref[rs, qc]
        sp = lax.dot_general(q, pref[ps, kc], nt, preferred_element_type=F32) * scale
        sc = lax.dot_general(q, cur_ref[rs, kc], nt, preferred_element_type=F32) * scale
        scores.append((jnp.where(p_ok, sp, NEG), jnp.where(cur_ok, sc, NEG)))
    maxes = [jnp.maximum(jnp.max(sp, axis=-1, keepdims=True), jnp.max(sc, axis=-1, keepdims=True))
             for sp, sc in scores]
    probs = [(jnp.exp(sp - m), jnp.exp(sc - m)) for (sp, sc), m in zip(scores, maxes)]
    dens = [jnp.sum(pp, axis=-1, keepdims=True) + jnp.sum(pc, axis=-1, keepdims=True) for pp, pc in probs]
    for (rs, qc, kc, vc, pref, ps, p_ok), (pp, pc), m, den in zip(tiles, probs, maxes, dens):
        o = jnp.dot((pp / den).astype(BF16), pref[ps, vc], preferred_element_type=F32)
        o += jnp.dot((pc / den).astype(BF16), cur_ref[rs, vc], preferred_element_type=F32)
        o_ref[rs, qc] = o
        lse_ref[rs, qc] = jnp.broadcast_to(m + jnp.log(den), (SUB_BLOCK, HEAD_DIM))
    if host is not None:
        n_steps = pl.num_programs(0) * pl.num_programs(1) * pl.num_programs(2)
        _hosted_end(host, step, n_steps, geometry)


def _prompt_attention(a, g, hosted=None):
    batch, dil, sub_len, _ = a.shape
    rows = min(PROMPT_ATTN_ROWS, sub_len)
    per_step = rows // SUB_BLOCK
    grid = (batch, dil, sub_len // rows)
    out_spec = pl.BlockSpec((None, None, rows, GROUP_W), lambda b, r, j: (b, r, j, 0))
    out_shape = jax.ShapeDtypeStruct((batch, dil, sub_len, GROUP_W), F32)
    in_specs = [pl.BlockSpec((None, None, rows, QKV_W), lambda b, r, j: (b, r, j, 0)),
                pl.BlockSpec((None, None, SUB_BLOCK, QKV_W),
                             lambda b, r, j: (b, r, jnp.maximum(j * per_step - 1, 0), 0))]
    out_specs, out_shapes, scratch, operands = [out_spec, out_spec], [out_shape, out_shape], [], [a, a]
    geometry = None
    if hosted is not None:
        plan = _host_plan(*hosted, grid[0] * grid[1] * grid[2])
        geometry = plan["geometry"]
        in_specs, out_specs, out_shapes, scratch, operands = _with_host(
            plan, in_specs, out_specs, out_shapes, scratch, operands)
    outs = pl.pallas_call(
        functools.partial(_prompt_attn_kernel, rows=rows, geometry=geometry),
        grid=grid,
        in_specs=in_specs,
        out_specs=out_specs,
        out_shape=out_shapes,
        scratch_shapes=scratch,
        compiler_params=pltpu.CompilerParams(dimension_semantics=("arbitrary", "arbitrary", "arbitrary")),
        name=f"prompt_attn_g{g}",
    )(*operands)
    if hosted is None:
        return outs
    return outs[0], outs[1], outs[2].reshape(plan["cache_shape"])


CONV_CHUNK_ROWS = 64
CONV_CHUNK_COLS = 256
CONV_PAD_ROWS = 16


def _prompt_conv_kernel(*refs, tiles_per_seq, tm, geometry):
    ins, outs, scratch, host = _split_hosted(refs, 4, 1, geometry is not None)
    u_ref, halo_ref, w_ref, b_ref = ins
    (c_ref,) = outs
    hist_ref, part_ref = scratch
    step = pl.program_id(0)
    if host is not None:
        _hosted_begin(host, step, geometry)
    first = (step % tiles_per_seq) == 0
    hist_ref[0:HALO_ROWS, :] = jnp.where(first, 0.0, _round_bf16(halo_ref[...]))
    hist_ref[HALO_ROWS:HALO_ROWS + tm, :] = _round_bf16(u_ref[...])
    hist_ref[HALO_ROWS + tm:, :] = jnp.zeros((CONV_PAD_ROWS, D_MODEL), F32)
    lead = HALO_ROWS - CONV_HIST
    span = CONV_CHUNK_ROWS + SUBLANES
    for r0 in range(0, tm, CONV_CHUNK_ROWS):
        for c0 in range(0, D_MODEL, CONV_CHUNK_COLS):
            cs = slice(c0, c0 + CONV_CHUNK_COLS)
            for b in range(SUBLANES):
                part = jnp.zeros((span, CONV_CHUNK_COLS), F32)
                for a in range((CONV_WIDTH + lead + SUBLANES - 1) // SUBLANES):
                    j = SUBLANES * a + b - lead
                    if 0 <= j < CONV_WIDTH:
                        part = part + _round_bf16(w_ref[j:j + 1, cs]) * hist_ref[r0 + SUBLANES * a:
                                                                                  r0 + SUBLANES * a + span, cs]
                part_ref[b] = part
            acc = part_ref[0, 0:CONV_CHUNK_ROWS, :]
            for b in range(1, SUBLANES):
                acc = acc + part_ref[b, pl.ds(b, CONV_CHUNK_ROWS), :]
            c_ref[r0:r0 + CONV_CHUNK_ROWS, cs] = acc + b_ref[:, cs]
    if host is not None:
        _hosted_end(host, step, pl.num_programs(0), geometry)


def _prompt_conv(u, w_dw, b_dw, seq, hosted):
    tm = PROMPT_CONV_ROWS
    n = u.shape[0]
    halo_per_tile = tm // HALO_ROWS
    plan = _host_plan(*hosted, n // tm)
    kernel = functools.partial(_prompt_conv_kernel, tiles_per_seq=seq // tm, tm=tm, geometry=plan["geometry"])
    in_specs, out_specs, out_shapes, scratch, operands = _with_host(
        plan,
        [pl.BlockSpec((tm, D_MODEL), lambda i: (i, 0)),
         pl.BlockSpec((HALO_ROWS, D_MODEL), lambda i: (jnp.maximum(i * halo_per_tile - 1, 0), 0)),
         pl.BlockSpec(w_dw.shape, lambda i: (0, 0)),
         pl.BlockSpec(b_dw.shape, lambda i: (0, 0))],
        [pl.BlockSpec((tm, D_MODEL), lambda i: (i, 0))],
        [jax.ShapeDtypeStruct((n, D_MODEL), F32)],
        [pltpu.VMEM((HALO_ROWS + tm + CONV_PAD_ROWS, D_MODEL), F32),
         pltpu.VMEM((SUBLANES, CONV_CHUNK_ROWS + SUBLANES, CONV_CHUNK_COLS), F32)],
        [u, u, w_dw, b_dw])
    c_pre, shifted = pl.pallas_call(
        kernel,
        grid=(n // tm,),
        in_specs=in_specs,
        out_specs=out_specs,
        out_shape=out_shapes,
        scratch_shapes=scratch,
        compiler_params=pltpu.CompilerParams(dimension_semantics=("arbitrary",)),
        name="prompt_conv",
    )(*operands)
    return c_pre, shifted.reshape(plan["cache_shape"])


def _sample_conv_kernel(state_ref, u_ref, w_ref, b_ref, new_state_ref, c_ref):
    st = state_ref[...]
    u = u_ref[...]
    c = jnp.sum(st * w_ref[0:CONV_HIST, :][None], axis=1)
    c_ref[...] = c + w_ref[CONV_HIST:CONV_WIDTH, :] * u + b_ref[...]
    new_state_ref[:, 0:CONV_HIST - 1, :] = state_ref[:, 1:CONV_HIST, :]
    new_state_ref[:, CONV_HIST - 1, :] = u


def _sample_conv(state, u, w_dw, b_dw):
    bb = SAMPLE_CONV_SEQS
    nb = state.shape[0]
    return pl.pallas_call(
        _sample_conv_kernel,
        grid=(nb // bb,),
        in_specs=[pl.BlockSpec((bb, CONV_HIST, D_MODEL), lambda i: (i, 0, 0)),
                  pl.BlockSpec((bb, D_MODEL), lambda i: (i, 0)),
                  pl.BlockSpec(w_dw.shape, lambda i: (0, 0)),
                  pl.BlockSpec(b_dw.shape, lambda i: (0, 0))],
        out_specs=[pl.BlockSpec((bb, CONV_HIST, D_MODEL), lambda i: (i, 0, 0)),
                   pl.BlockSpec((bb, D_MODEL), lambda i: (i, 0))],
        out_shape=[jax.ShapeDtypeStruct(state.shape, F32), jax.ShapeDtypeStruct((nb, D_MODEL), F32)],
        compiler_params=pltpu.CompilerParams(dimension_semantics=("parallel",)),
        name="sample_conv",
    )(state, u, w_dw, b_dw)


def _sample_attn_kernel(q_ref, kn_ref, vn_ref, kv_ref, o_ref, lse_ref):
    scale = HEAD_DIM ** -0.5
    q = _round_bf16(q_ref[...])
    k = _round_bf16(kv_ref[:, :, 0])
    v = _round_bf16(kv_ref[:, :, 1])
    s = jnp.sum(k * q[:, None], axis=-1, keepdims=True) * scale
    sn = jnp.sum(_round_bf16(kn_ref[...]) * q, axis=-1, keepdims=True) * scale
    m = jnp.maximum(jnp.max(s, axis=1), sn)
    p = jnp.exp(s - m[:, None])
    pn = jnp.exp(sn - m)
    den = jnp.sum(p, axis=1) + pn
    o = (jnp.sum(_round_bf16(p / den[:, None]) * v, axis=1)
         + _round_bf16(pn / den) * _round_bf16(vn_ref[...]))
    o_ref[...] = o
    lse_ref[...] = jnp.broadcast_to(m + jnp.log(den), o.shape)


def _sample_attention(q, kn, vn, cache, g, window, dil):
    bb = SAMPLE_ATTN_SEQS
    n_seq = q.shape[0]
    strided = cache.reshape(n_seq, window // dil, dil, 2, HEADS_PER_GROUP, HEAD_DIM)
    head_spec = pl.BlockSpec((bb, HEADS_PER_GROUP, HEAD_DIM), lambda i: (i, 0, 0))
    return pl.pallas_call(
        _sample_attn_kernel,
        grid=(n_seq // bb,),
        in_specs=[head_spec, head_spec, head_spec,
                  pl.BlockSpec((bb, window // dil, None, 2, HEADS_PER_GROUP, HEAD_DIM),
                               lambda i: (i, 0, 0, 0, 0, 0))],
        out_specs=[head_spec, head_spec],
        out_shape=[jax.ShapeDtypeStruct(q.shape, F32), jax.ShapeDtypeStruct(q.shape, F32)],
        compiler_params=pltpu.CompilerParams(dimension_semantics=("parallel",)),
        name=f"sample_attn_g{g}",
    )(q, kn, vn, strided)


SHIFT_CHUNK_ROWS = 256


def _shift_geometry(n_seq, window):
    n_copy = n_seq * window - 1
    n_full = n_copy // SHIFT_CHUNK_ROWS
    return n_full, n_copy - n_full * SHIFT_CHUNK_ROWS


def _shift_in(src, buf_slot, sem, c):
    return pltpu.make_async_copy(src.at[pl.ds(c * SHIFT_CHUNK_ROWS + 1, SHIFT_CHUNK_ROWS)], buf_slot, sem)


def _shift_out(dst, buf_slot, sem, c):
    return pltpu.make_async_copy(buf_slot, dst.at[pl.ds(c * SHIFT_CHUNK_ROWS, SHIFT_CHUNK_ROWS)], sem)


def _shift_finish(src, fresh, dst, buf_slot, sem_a, sem_b, sem_fresh, *, window, n_seq):
    n_full, tail = _shift_geometry(n_seq, window)
    if tail:
        start = n_full * SHIFT_CHUNK_ROWS
        tail_in = pltpu.make_async_copy(src.at[pl.ds(start + 1, tail)], buf_slot.at[pl.ds(0, tail)], sem_a)
        tail_in.start()
        tail_in.wait()
        tail_out = pltpu.make_async_copy(buf_slot.at[pl.ds(0, tail)], dst.at[pl.ds(start, tail)], sem_b)
        tail_out.start()
        tail_out.wait()

    def fresh_copy(b):
        return pltpu.make_async_copy(fresh.at[b], dst.at[pl.ds(b * window + window - 1, 1)], sem_fresh)

    def start_fresh(b, carry):
        fresh_copy(b).start()
        return carry

    def wait_fresh(b, carry):
        fresh_copy(b).wait()
        return carry

    lax.fori_loop(0, n_seq, start_fresh, 0)
    lax.fori_loop(0, n_seq, wait_fresh, 0)


def _flat_cache(cache):
    _, n_seq, window = cache.shape[:3]
    return cache.reshape((n_seq * window,) + cache.shape[3:])


def _host_plan(cache, fresh, n_steps):
    _, n_seq, window = cache.shape[:3]
    flat = _flat_cache(cache)
    n_full, _ = _shift_geometry(n_seq, window)
    per_step = pl.cdiv(n_full, n_steps)
    buf_shape = (2, per_step, SHIFT_CHUNK_ROWS) + flat.shape[1:]
    any_spec = pl.BlockSpec(memory_space=pl.ANY)
    return dict(
        operands=[flat, fresh], in_specs=[any_spec, any_spec], out_spec=any_spec,
        out_shape=jax.ShapeDtypeStruct(flat.shape, flat.dtype),
        scratch=[pltpu.VMEM(buf_shape, flat.dtype), pltpu.SemaphoreType.DMA((2, per_step)),
                 pltpu.SemaphoreType.DMA((2, per_step)), pltpu.SemaphoreType.DMA((1,))],
        scratch_bytes=_nbytes(buf_shape, flat.dtype), geometry=dict(window=window, n_seq=n_seq),
        cache_shape=cache.shape)


def _hosted_shift_step(src, dst, buf, sem_in, sem_out, step, *, window, n_seq):
    n_full, _ = _shift_geometry(n_seq, window)
    per_step = buf.shape[1]
    cur = step % 2
    prv = 1 - cur
    for q in range(per_step):
        c = step * per_step + q
        c1 = c - per_step
        c2 = c - 2 * per_step

        @pl.when((c1 >= 0) & (c1 < n_full))
        def _():
            _shift_in(src, buf.at[prv, q], sem_in.at[prv, q], c1).wait()
            _shift_out(dst, buf.at[prv, q], sem_out.at[prv, q], c1).start()

        @pl.when((c2 >= 0) & (c2 < n_full))
        def _():
            _shift_out(dst, buf.at[cur, q], sem_out.at[cur, q], c2).wait()

        @pl.when(c < n_full)
        def _():
            _shift_in(src, buf.at[cur, q], sem_in.at[cur, q], c).start()


def _hosted_shift_drain(src, fresh, dst, buf, sem_in, sem_out, sem_fresh, step, *, window, n_seq):
    n_full, _ = _shift_geometry(n_seq, window)
    per_step = buf.shape[1]
    cur = step % 2
    prv = 1 - cur
    for q in range(per_step):
        c = step * per_step + q
        c1 = c - per_step

        @pl.when(c < n_full)
        def _():
            _shift_in(src, buf.at[cur, q], sem_in.at[cur, q], c).wait()
            _shift_out(dst, buf.at[cur, q], sem_out.at[cur, q], c).start()

        @pl.when((c1 >= 0) & (c1 < n_full))
        def _():
            _shift_out(dst, buf.at[prv, q], sem_out.at[prv, q], c1).wait()

        @pl.when(c < n_full)
        def _():
            _shift_out(dst, buf.at[cur, q], sem_out.at[cur, q], c).wait()

    _shift_finish(src, fresh, dst, buf.at[0, 0], sem_in.at[0, 0], sem_out.at[0, 0], sem_fresh.at[0],
                  window=window, n_seq=n_seq)


def _split_hosted(refs, n_in, n_out, hosting):
    if not hosting:
        return refs[:n_in], refs[n_in:n_in + n_out], refs[n_in + n_out:], None
    ins = refs[:n_in]
    src, fresh = refs[n_in:n_in + 2]
    outs = refs[n_in + 2:n_in + 2 + n_out]
    dst = refs[n_in + 2 + n_out]
    rest = refs[n_in + 3 + n_out:]
    buf, sem_in, sem_out, sem_fresh = rest[-4:]
    return ins, outs, rest[:-4], (src, fresh, dst, buf, sem_in, sem_out, sem_fresh)


def _hosted_begin(host, step, geometry):
    src, _, dst, buf, sem_in, sem_out, _ = host
    _hosted_shift_step(src, dst, buf, sem_in, sem_out, step, **geometry)


def _hosted_end(host, step, n_steps, geometry):
    @pl.when(step == n_steps - 1)
    def _():
        _hosted_shift_drain(*host, step, **geometry)


def _with_host(plan, in_specs, out_specs, out_shapes, scratch, operands):
    return (in_specs + plan["in_specs"], out_specs + [plan["out_spec"]], out_shapes + [plan["out_shape"]],
            scratch + plan["scratch"], operands + plan["operands"])


def _merge_kernel(x_ref, o0_ref, o1_ref, o2_ref, l0_ref, l1_ref, l2_ref, c_ref, sg_ref, cng_ref, cnb_ref,
                  wb_ref, wo_ref, gf_ref, wr_ref, br_ref, x1_ref, t_ref, cw_ref, order_ref):
    def token_order(ref, k):
        dil = ref.shape[0]
        if dil == 1:
            return ref[0]
        for r in range(dil):
            for h in range(HEADS_PER_GROUP):
                order_ref[k, h, pl.ds(r, ref.shape[1], stride=dil), :] = ref[r, :, h * HEAD_DIM:(h + 1) * HEAD_DIM]
        return jnp.concatenate([order_ref[k, h] for h in range(HEADS_PER_GROUP)], axis=-1)

    l0, l1, l2 = token_order(l0_ref, 0), token_order(l1_ref, 1), token_order(l2_ref, 2)
    lm = jnp.maximum(jnp.maximum(l0, l1), l2)
    e0, e1, e2 = jnp.exp(l0 - lm), jnp.exp(l1 - lm), jnp.exp(l2 - lm)
    es = e0 + e1 + e2
    o_attn = (e0 / es) * token_order(o0_ref, 3)
    o_attn = o_attn + (e1 / es) * token_order(o1_ref, 4)
    o_attn = o_attn + (e2 / es) * token_order(o2_ref, 5)

    cf = c_ref[...]
    mu = jnp.mean(cf, axis=-1, keepdims=True)
    var = jnp.mean(jnp.square(cf - mu), axis=-1, keepdims=True)
    cn = ((cf - mu) * lax.rsqrt(var + EPS)) * cng_ref[...] + cnb_ref[...]
    c = cn * _sigmoid(cn)

    ya = jnp.dot(o_attn.astype(BF16), wb_ref[0:GROUP_W, :], preferred_element_type=F32)
    yb = jnp.dot(c.astype(BF16), wb_ref[GROUP_W:, :], preferred_element_type=F32)
    mix = sg_ref[:, 0:D_MODEL] * ya + sg_ref[:, D_MODEL:] * yb
    x1 = x_ref[...] + jnp.dot(mix.astype(BF16), wo_ref[...], preferred_element_type=F32)
    x1_ref[...] = x1

    t = _rms_normed(x1, gf_ref[...]).astype(BF16)
    t_ref[...] = t
    logits = jnp.dot(t, wr_ref[...], preferred_element_type=F32) + br_ref[...]
    lane = lax.broadcasted_iota(jnp.int32, logits.shape, 1)
    lane_f = lane.astype(F32)
    big = float(LANES)
    is_group = (lane >= N_EXPERTS) & (lane < N_EXPERTS + N_COARSE)
    gl = jnp.where(is_group, logits, -jnp.inf)
    gmax = jnp.max(gl, axis=-1, keepdims=True)
    gsel = jnp.min(jnp.where(gl == gmax, lane_f, big), axis=-1, keepdims=True) - float(N_EXPERTS)
    gw = 1.0 / jnp.sum(jnp.where(is_group, jnp.exp(gl - gmax), 0.0), axis=-1, keepdims=True)
    in_group = (lane < N_EXPERTS) & ((lane // EXPERTS_PER_GROUP).astype(F32) == gsel)
    el = jnp.where(in_group, logits, -jnp.inf)
    v1 = jnp.max(el, axis=-1, keepdims=True)
    i1 = jnp.min(jnp.where(el == v1, lane_f, big), axis=-1, keepdims=True)
    el2 = jnp.where(lane_f == i1, -jnp.inf, el)
    v2 = jnp.max(el2, axis=-1, keepdims=True)
    i2 = jnp.min(jnp.where(el2 == v2, lane_f, big), axis=-1, keepdims=True)
    e2nd = jnp.exp(v2 - v1)
    tden = 1.0 + e2nd
    cw_ref[...] = jnp.where(lane_f == i1, (1.0 / tden) * gw, jnp.where(lane_f == i2, (e2nd / tden) * gw, 0.0))


def _merge(x, outs, lses, c_pre, sg, consts, batch, tm, tag):
    n = x.shape[0]
    tiles = n // (batch * tm)
    rows = [x, c_pre, sg]
    row_spec = lambda a: pl.BlockSpec((tm, a.shape[1]), lambda b, i: (b * tiles + i, 0))
    grp_spec = lambda a: pl.BlockSpec((None, a.shape[1], tm // a.shape[1], GROUP_W), lambda b, i: (b, 0, i, 0))
    in_specs = ([row_spec(x)] + [grp_spec(a) for a in outs] + [grp_spec(a) for a in lses]
                + [row_spec(c_pre), row_spec(sg)] + [pl.BlockSpec(c.shape, lambda b, i: (0, 0)) for c in consts])
    out_shapes = [jax.ShapeDtypeStruct((n, D_MODEL), F32), jax.ShapeDtypeStruct((n, D_MODEL), BF16),
                  jax.ShapeDtypeStruct((n, LANES), F32)]
    out_specs = [row_spec(s) for s in out_shapes]
    block_bytes = sum(_nbytes((tm, a.shape[1]), a.dtype) for a in rows + out_shapes)
    block_bytes += 6 * _nbytes((tm, GROUP_W), F32) + sum(_nbytes(c.shape, c.dtype) for c in consts)
    return pl.pallas_call(
        _merge_kernel,
        grid=(batch, tiles),
        in_specs=in_specs,
        out_specs=out_specs,
        out_shape=out_shapes,
        scratch_shapes=[pltpu.VMEM((2 * N_GROUPS, HEADS_PER_GROUP, tm, HEAD_DIM), F32)],
        compiler_params=pltpu.CompilerParams(
            dimension_semantics=("parallel", "parallel"),
            vmem_limit_bytes=_vmem_limit(block_bytes, _nbytes((2 * N_GROUPS, tm, GROUP_W), F32))),
        name="merge_" + tag,
    )(x, *outs, *lses, c_pre, sg, *consts)


def _moe_kernel(*refs, geometry):
    ins, outs, _, host = _split_hosted(refs, 7, 1, geometry is not None)
    x1_ref, t_ref, cw_ref, wg_ref, wu_ref, wd_ref, gn_ref = ins
    (y_ref,) = outs
    e = pl.program_id(1)
    n_e = pl.num_programs(1)
    step = pl.program_id(0) * n_e + e
    if host is not None:
        _hosted_begin(host, step, geometry)

    t = t_ref[...]
    gate = jnp.dot(t, wg_ref[...], preferred_element_type=F32)
    up = jnp.dot(t, wu_ref[...], preferred_element_type=F32)
    lane = lax.broadcasted_iota(jnp.int32, cw_ref.shape, 1)
    w = jnp.sum(jnp.where(lane == e, cw_ref[...], 0.0), axis=-1, keepdims=True)
    h = ((gate * _sigmoid(gate)) * up) * w
    part = jnp.dot(h.astype(BF16), wd_ref[...], preferred_element_type=F32)

    @pl.when(e == 0)
    def _():
        y_ref[...] = part

    @pl.when(e > 0)
    def _():
        y_ref[...] += part

    @pl.when(e == n_e - 1)
    def _():
        y_ref[...] = _rms_normed(x1_ref[...] + y_ref[...], gn_ref[...])

    if host is not None:
        _hosted_end(host, step, pl.num_programs(0) * n_e, geometry)


def _moe_final(x1, t, cw, w_gate, w_up, w_down, g_final, tm, tag, hosted=None):
    n = x1.shape[0]
    grid = (n // tm, N_EXPERTS)
    row = lambda w: pl.BlockSpec((tm, w), lambda i, e: (i, 0))
    once = lambda w: pl.BlockSpec((tm, w), lambda i, e: (i, 0), pipeline_mode=pl.Buffered(1))
    in_specs = [once(D_MODEL), once(D_MODEL), once(LANES),
                pl.BlockSpec((None, D_MODEL, D_EXPERT), lambda i, e: (e, 0, 0)),
                pl.BlockSpec((None, D_MODEL, D_EXPERT), lambda i, e: (e, 0, 0)),
                pl.BlockSpec((None, D_EXPERT, D_MODEL), lambda i, e: (e, 0, 0)),
                pl.BlockSpec(g_final.shape, lambda i, e: (0, 0))]
    out_specs = [row(D_MODEL)]
    out_shapes = [jax.ShapeDtypeStruct((n, D_MODEL), F32)]
    scratch = []
    operands = [x1, t, cw, w_gate, w_up, w_down, g_final]
    scratch_bytes = 0
    geometry = None
    if hosted is not None:
        plan = _host_plan(*hosted, grid[0] * grid[1])
        geometry = plan["geometry"]
        in_specs += plan["in_specs"]
        out_specs.append(plan["out_spec"])
        out_shapes.append(plan["out_shape"])
        scratch += plan["scratch"]
        operands += plan["operands"]
        scratch_bytes += plan["scratch_bytes"]
    block_bytes = (_nbytes((tm, D_MODEL), F32) * 2 + _nbytes((tm, D_MODEL), BF16) + _nbytes((tm, LANES), F32)
                   + 3 * _nbytes((D_MODEL, D_EXPERT), BF16))
    outs = pl.pallas_call(
        functools.partial(_moe_kernel, geometry=geometry),
        grid=grid,
        in_specs=in_specs,
        out_specs=out_specs,
        out_shape=out_shapes,
        scratch_shapes=scratch,
        compiler_params=pltpu.CompilerParams(
            dimension_semantics=("arbitrary", "arbitrary"),
            vmem_limit_bytes=_vmem_limit(block_bytes, scratch_bytes)),
        name="moe_" + tag,
    )(*operands)
    if hosted is None:
        return outs[0]
    return outs[0], outs[1].reshape(hosted[0].shape)


def kernel(x_prompt, x_sample, cache_kv_w128, cache_kv_w512, cache_kv_w2048, state_conv, norm_mix, norm_ffn,
           norm_final, w_in, w_dw, b_dw, conv_norm_g, conv_norm_b, w_branch, w_out, w_route_group,
           b_route_group, w_route_expert, b_route_expert, w_exp_gate, w_exp_up, w_exp_down):
    depth = w_in.shape[0]
    assert depth == 1, "single-layer step"
    batch, seq, _ = x_prompt.shape
    n_seq, dec_seq, _ = x_sample.shape
    assert dec_seq == 1
    n_prompt = batch * seq

    w_in_b = w_in[0].astype(BF16)
    w_qkv = w_in_b[:, :3 * ATTN_W]
    w_glu = w_in_b[:, 3 * ATTN_W:3 * ATTN_W + 2 * D_MODEL]
    w_gate = w_in_b[:, 3 * ATTN_W + 2 * D_MODEL:]
    g_mix = norm_mix[0][None]
    g_ffn = norm_ffn[0][None]
    g_final = norm_final[None]
    w_dw0, b_dw0 = w_dw[0], b_dw[0][None]
    w_router = jnp.concatenate([w_route_expert[0], w_route_group[0]], axis=1)
    w_router = jnp.pad(w_router, ((0, 0), (0, LANES - w_router.shape[1]))).astype(BF16)
    b_router = jnp.concatenate([b_route_expert[0], b_route_group[0]])
    b_router = jnp.pad(b_router, (0, LANES - b_router.shape[0]))[None]
    merge_consts = (conv_norm_g[0][None], conv_norm_b[0][None], w_branch[0].astype(BF16), w_out[0].astype(BF16),
                    g_ffn, w_router, b_router)
    moe_w = (w_exp_gate[0].astype(BF16), w_exp_up[0].astype(BF16), w_exp_down[0].astype(BF16))
    caches = (cache_kv_w128, cache_kv_w512, cache_kv_w2048)

    xs = x_sample.reshape(n_seq, D_MODEL)
    (qkvf,) = _row_call(_qkv_sample_kernel, xs, (g_mix, w_qkv), [jax.ShapeDtypeStruct((n_seq, 3 * ATTN_W), F32)],
                        n_seq, "proj_qkv_s")
    u_s, sg_s = _glu_and_gates(xs, g_mix, w_glu, w_gate, n_seq, "s")
    qkv5 = qkvf.reshape(n_seq, 3, N_GROUPS, HEADS_PER_GROUP, HEAD_DIM)
    outs_s, lses_s, fresh = [], [], []
    for g, (win, dil) in enumerate(DILATED_GROUPS):
        q, kn, vn = qkv5[:, 0, g], qkv5[:, 1, g], qkv5[:, 2, g]
        o, l = _sample_attention(q, kn, vn, caches[g], g, win, dil)
        outs_s.append(o.reshape(1, 1, n_seq, GROUP_W))
        lses_s.append(l.reshape(1, 1, n_seq, GROUP_W))
        fresh.append(jnp.stack([kn, vn], axis=1)[:, None])
    kv_s = [None] * N_GROUPS

    xp = x_prompt.reshape(n_prompt, D_MODEL)
    a0, a1, a2, t0, t1, t2 = _qkv_prompt(xp, g_mix, w_qkv, batch, seq)
    u_p, sg_p = _glu_and_gates(xp, g_mix, w_glu, w_gate, PROMPT_PROJ_ROWS, "p")
    o0, l0, kv_s[0] = _prompt_attention(a0, 0, hosted=(caches[0], fresh[0]))
    o1, l1 = _prompt_attention(a1, 1)
    o2, l2 = _prompt_attention(a2, 2)
    c_pre, kv_s[1] = _prompt_conv(u_p, w_dw0, b_dw0, seq, hosted=(caches[1], fresh[1]))
    x1, t, cw = _merge(xp, (o0, o1, o2), (l0, l1, l2), c_pre, sg_p, merge_consts, batch, PROMPT_MERGE_ROWS, "p")
    y_prompt, kv_s[2] = _moe_final(x1, t, cw, *moe_w, g_final, PROMPT_MOE_ROWS, "p",
                                   hosted=(caches[2], fresh[2]))
    y_prompt = y_prompt.reshape(batch, seq, D_MODEL)
    kv_p = [tail.reshape(batch, tail.shape[1], 2, HEADS_PER_GROUP, HEAD_DIM)[None] for tail in (t0, t1, t2)]
    conv_p = u_p.reshape(batch, seq, D_MODEL)[:, seq - CONV_HIST:][None]

    conv_s, c_pre_s = _sample_conv(state_conv[0], u_s, w_dw0, b_dw0)
    x1s, ts, cws = _merge(xs, outs_s, lses_s, c_pre_s, sg_s, merge_consts, 1, n_seq, "s")
    y_sample = _moe_final(x1s, ts, cws, *moe_w, g_final, n_seq, "s").reshape(n_seq, 1, D_MODEL)

    return (y_prompt, y_sample, kv_p[0], kv_p[1], kv_p[2], conv_p, kv_s[0], kv_s[1], kv_s[2], conv_s[None])
```

```python
import functools

import jax
import jax.numpy as jnp
from jax import lax
from jax.experimental import pallas as pl
from jax.experimental.pallas import tpu as pltpu

D_MODEL = 1024
HEAD_DIM = 128
HEADS_PER_GROUP = 4
GROUP_W = HEADS_PER_GROUP * HEAD_DIM
DILATED_GROUPS = ((128, 1), (512, 4), (2048, 16))
N_GROUPS = len(DILATED_GROUPS)
ATTN_W = N_GROUPS * GROUP_W
QKV_W = 3 * GROUP_W
SUB_BLOCK = 128
LSE_LANES = 32
LSE_W = HEADS_PER_GROUP * LSE_LANES
CONV_WIDTH = 31
CONV_HIST = CONV_WIDTH - 1
N_COARSE = 4
EXPERTS_PER_GROUP = 8
N_EXPERTS = N_COARSE * EXPERTS_PER_GROUP
D_EXPERT = 256
EPS = 1e-6
NEG = -1e30
LANES = 128
SUBLANES = 8
HALO_ROWS = 32
V7X_VMEM_BYTES = 64 * 1024 * 1024

PROMPT_PROJ_ROWS = 512
PROMPT_ATTN_ROWS = 512
PROMPT_CONV_ROWS = 256
PROMPT_MERGE_ROWS = 256
PROMPT_MOE_ROWS = 2048
SAMPLE_ATTN_SEQS = 8
SAMPLE_CONV_SEQS = 32

F32 = jnp.float32
BF16 = jnp.bfloat16


def _vmem_limit(block_bytes, scratch_bytes=0):
    need = 2 * block_bytes + scratch_bytes
    return int(min(need + 16 * 1024 * 1024, V7X_VMEM_BYTES - 8 * 1024 * 1024))


def _nbytes(shape, dtype):
    n = 1
    for s in shape:
        n *= s
    return n * jnp.dtype(dtype).itemsize


def _rms_normed(x, g):
    r = lax.rsqrt(jnp.mean(x * x, axis=-1, keepdims=True) + EPS)
    return (x * r) * g


def _sigmoid(x):
    return 1.0 / (1.0 + jnp.exp(-x))


def _round_bf16(x):
    return x.astype(BF16).astype(F32)


PROJ_CHUNK = 512


def _qkv_prompt_kernel(x_ref, g_ref, w_ref, a0_ref, a1_ref, a2_ref, t0_ref, t1_ref, t2_ref, stage_ref, *, tm):
    xb = _rms_normed(x_ref[...], g_ref[...]).astype(BF16)
    group_refs = (a0_ref, a1_ref, a2_ref)
    tail_refs = (t0_ref, t1_ref, t2_ref)
    for part in range(3):
        for g, (win, dil) in enumerate(DILATED_GROUPS):
            c0 = (part * N_GROUPS + g) * GROUP_W
            acc = jnp.dot(xb, w_ref[:, c0:c0 + GROUP_W], preferred_element_type=F32)
            cols = slice(part * GROUP_W, (part + 1) * GROUP_W)
            if dil == 1:
                group_refs[g][0, :, cols] = acc.astype(BF16)
            else:
                for h in range(HEADS_PER_GROUP):
                    stage_ref[h] = acc[:, h * HEAD_DIM:(h + 1) * HEAD_DIM]
                for r in range(dil):
                    for h in range(HEADS_PER_GROUP):
                        hc = slice(cols.start + h * HEAD_DIM, cols.start + (h + 1) * HEAD_DIM)
                        group_refs[g][r, :, hc] = stage_ref[h, pl.ds(r, tm // dil, stride=dil), :].astype(BF16)
            if part >= 1:
                keep = min(win, tm)
                tcols = slice((part - 1) * GROUP_W, part * GROUP_W)
                tail_refs[g][:, tcols] = acc[tm - keep:, :]


def _qkv_prompt(x, g_mix, w_qkv, batch, seq):
    tm = PROMPT_PROJ_ROWS
    tiles = seq // tm
    assert all(win % tm == 0 or tm % win == 0 for win, _ in DILATED_GROUPS)
    group_shapes = [jax.ShapeDtypeStruct((batch, dil, seq // dil, QKV_W), BF16) for _, dil in DILATED_GROUPS]
    group_specs = [pl.BlockSpec((None, dil, tm // dil, QKV_W), lambda b, i: (b, 0, i, 0)) for _, dil in DILATED_GROUPS]
    tail_shapes, tail_specs = [], []
    for win, _ in DILATED_GROUPS:
        keep = min(win, seq)
        rows = min(keep, tm)
        first = (seq - keep) // tm
        tail_shapes.append(jax.ShapeDtypeStruct((batch, keep, 2 * GROUP_W), F32))
        tail_specs.append(pl.BlockSpec((None, rows, 2 * GROUP_W),
                                       lambda b, i, first=first: (b, jnp.maximum(i - first, 0), 0)))
    block_bytes = (_nbytes((tm, D_MODEL), F32) + _nbytes(w_qkv.shape, BF16) + 3 * _nbytes((tm, QKV_W), BF16)
                   + 3 * _nbytes((tm, 2 * GROUP_W), F32))
    return pl.pallas_call(
        functools.partial(_qkv_prompt_kernel, tm=tm),
        grid=(batch, tiles),
        in_specs=[pl.BlockSpec((tm, D_MODEL), lambda b, i: (b * tiles + i, 0)),
                  pl.BlockSpec(g_mix.shape, lambda b, i: (0, 0)),
                  pl.BlockSpec(w_qkv.shape, lambda b, i: (0, 0))],
        out_specs=group_specs + tail_specs,
        out_shape=group_shapes + tail_shapes,
        scratch_shapes=[pltpu.VMEM((HEADS_PER_GROUP, tm, HEAD_DIM), F32)],
        compiler_params=pltpu.CompilerParams(
            dimension_semantics=("arbitrary", "arbitrary"),
            vmem_limit_bytes=_vmem_limit(block_bytes, _nbytes((tm, GROUP_W), F32))),
        name="proj_qkv_p",
    )(x, g_mix, w_qkv)


def _qkv_sample_kernel(x_ref, g_ref, w_ref, qkvf_ref):
    xb = _rms_normed(x_ref[...], g_ref[...]).astype(BF16)
    for c in range(0, 3 * ATTN_W, PROJ_CHUNK):
        qkvf_ref[:, c:c + PROJ_CHUNK] = jnp.dot(xb, w_ref[:, c:c + PROJ_CHUNK], preferred_element_type=F32)


def _glu_gate_kernel(x_ref, g_ref, w_ref, u_ref, sg_ref):
    xb = _rms_normed(x_ref[...], g_ref[...]).astype(BF16)
    for c in range(0, D_MODEL, PROJ_CHUNK):
        a = jnp.dot(xb, w_ref[:, c:c + PROJ_CHUNK], preferred_element_type=F32)
        b = jnp.dot(xb, w_ref[:, D_MODEL + c:D_MODEL + c + PROJ_CHUNK], preferred_element_type=F32)
        u_ref[:, c:c + PROJ_CHUNK] = a * _sigmoid(b)
    for c in range(0, 2 * D_MODEL, PROJ_CHUNK):
        acc = jnp.dot(xb, w_ref[:, 2 * D_MODEL + c:2 * D_MODEL + c + PROJ_CHUNK], preferred_element_type=F32)
        sg_ref[:, c:c + PROJ_CHUNK] = _sigmoid(acc)


def _row_call(kernel, x, consts, out_shapes, tm, name):
    n = x.shape[0]
    in_specs = [pl.BlockSpec((tm, x.shape[1]), lambda i: (i, 0))]
    in_specs += [pl.BlockSpec(c.shape, lambda i, nd=c.ndim: (0,) * nd) for c in consts]
    out_specs = [pl.BlockSpec((tm, s.shape[1]), lambda i: (i, 0)) for s in out_shapes]
    block_bytes = _nbytes((tm, x.shape[1]), x.dtype) + sum(_nbytes(c.shape, c.dtype) for c in consts)
    block_bytes += sum(_nbytes((tm, s.shape[1]), s.dtype) for s in out_shapes)
    return pl.pallas_call(
        kernel,
        grid=(n // tm,),
        in_specs=in_specs,
        out_specs=out_specs,
        out_shape=out_shapes,
        compiler_params=pltpu.CompilerParams(
            dimension_semantics=("parallel",), vmem_limit_bytes=_vmem_limit(block_bytes)),
        name=name,
    )(x, *consts)


def _glu_and_gates(x, g_mix, w_glu_gate, tm, tag):
    n = x.shape[0]
    return _row_call(_glu_gate_kernel, x, (g_mix, w_glu_gate),
                     [jax.ShapeDtypeStruct((n, D_MODEL), F32), jax.ShapeDtypeStruct((n, 2 * D_MODEL), F32)], tm,
                     "proj_glu_gate_" + tag)


def _prompt_attn_kernel(*refs, rows, geometry):
    ins, outs, _, host = _split_hosted(refs, 2, 2, geometry is not None)
    cur_ref, prev_ref = ins
    o_ref, lse_ref = outs
    j = pl.program_id(2)
    step = (pl.program_id(0) * pl.num_programs(1) + pl.program_id(1)) * pl.num_programs(2) + j
    if host is not None:
        _hosted_begin(host, step, geometry)
    row = lax.broadcasted_iota(jnp.int32, (SUB_BLOCK, SUB_BLOCK), 0)
    col = lax.broadcasted_iota(jnp.int32, (SUB_BLOCK, SUB_BLOCK), 1)
    cur_ok = col <= row
    prev_ok = col >= row
    first_ok = prev_ok & (j > 0)
    scale = HEAD_DIM ** -0.5
    nt = (((1,), (1,)), ((), ()))
    tiles = []
    for s in range(rows // SUB_BLOCK):
        rs = slice(s * SUB_BLOCK, (s + 1) * SUB_BLOCK)
        ps = slice((s - 1) * SUB_BLOCK, s * SUB_BLOCK)
        for h in range(HEADS_PER_GROUP):
            qc = slice(h * HEAD_DIM, (h + 1) * HEAD_DIM)
            kc = slice(GROUP_W + h * HEAD_DIM, GROUP_W + (h + 1) * HEAD_DIM)
            vc = slice(2 * GROUP_W + h * HEAD_DIM, 2 * GROUP_W + (h + 1) * HEAD_DIM)
            if s == 0:
                tiles.append((rs, qc, kc, vc, prev_ref, slice(None), first_ok))
            else:
                tiles.append((rs, qc, kc, vc, cur_ref, ps, prev_ok))
    scores = []
    for rs, qc, kc, vc, pref, ps, p_ok in tiles:
        q = cur_ref[rs, qc]
        sp = lax.dot_general(q, pref[ps, kc], nt, preferred_element_type=F32) * scale
        sc = lax.dot_general(q, cur_ref[rs, kc], nt, preferred_element_type=F32) * scale
        scores.append((jnp.where(p_ok, sp, NEG), jnp.where(cur_ok, sc, NEG)))
    maxes = [jnp.maximum(jnp.max(sp, axis=-1, keepdims=True), jnp.max(sc, axis=-1, keepdims=True))
             for sp, sc in scores]
    probs = [(jnp.exp(sp - m), jnp.exp(sc - m)) for (sp, sc), m in zip(scores, maxes)]
    dens = [jnp.sum(pp, axis=-1, keepdims=True) + jnp.sum(pc, axis=-1, keepdims=True) for pp, pc in probs]
    for (rs, qc, kc, vc, pref, ps, p_ok), (pp, pc), m, den in zip(tiles, probs, maxes, dens):
        o = jnp.dot((pp / den).astype(BF16), pref[ps, vc], preferred_element_type=F32)
        o += jnp.dot((pc / den).astype(BF16), cur_ref[rs, vc], preferred_element_type=F32)
        o_ref[rs, qc] = o
        h = qc.start // HEAD_DIM
        lse_ref[rs, h * LSE_LANES:(h + 1) * LSE_LANES] = jnp.broadcast_to(m + jnp.log(den), (SUB_BLOCK, LSE_LANES))
    if host is not None:
        n_steps = pl.num_programs(0) * pl.num_programs(1) * pl.num_programs(2)
        _hosted_end(host, step, n_steps, geometry)


def _prompt_attention(a, g, hosted=None):
    batch, dil, sub_len, _ = a.shape
    rows = min(PROMPT_ATTN_ROWS, sub_len)
    per_step = rows // SUB_BLOCK
    grid = (batch, dil, sub_len // rows)
    out_spec = pl.BlockSpec((None, None, rows, GROUP_W), lambda b, r, j: (b, r, j, 0))
    out_shape = jax.ShapeDtypeStruct((batch, dil, sub_len, GROUP_W), F32)
    in_specs = [pl.BlockSpec((None, None, rows, QKV_W), lambda b, r, j: (b, r, j, 0)),
                pl.BlockSpec((None, None, SUB_BLOCK, QKV_W),
                             lambda b, r, j: (b, r, jnp.maximum(j * per_step - 1, 0), 0))]
    lse_spec = pl.BlockSpec((None, None, rows, LSE_W), lambda b, r, j: (b, r, j, 0))
    lse_shape = jax.ShapeDtypeStruct((batch, dil, sub_len, LSE_W), F32)
    out_specs, out_shapes, scratch, operands = [out_spec, lse_spec], [out_shape, lse_shape], [], [a, a]
    geometry = None
    if hosted is not None:
        plan = _host_plan(*hosted, grid[0] * grid[1] * grid[2])
        geometry = plan["geometry"]
        in_specs, out_specs, out_shapes, scratch, operands = _with_host(
            plan, in_specs, out_specs, out_shapes, scratch, operands)
    outs = pl.pallas_call(
        functools.partial(_prompt_attn_kernel, rows=rows, geometry=geometry),
        grid=grid,
        in_specs=in_specs,
        out_specs=out_specs,
        out_shape=out_shapes,
        scratch_shapes=scratch,
        compiler_params=pltpu.CompilerParams(dimension_semantics=("arbitrary", "arbitrary", "arbitrary")),
        name=f"prompt_attn_g{g}",
    )(*operands)
    if hosted is None:
        return outs
    return outs[0], outs[1], outs[2].reshape(plan["cache_shape"])


CONV_CHUNK_ROWS = 64
CONV_CHUNK_COLS = 256
CONV_PAD_ROWS = 16


def _prompt_conv_kernel(*refs, tiles_per_seq, tm, geometry):
    ins, outs, scratch, host = _split_hosted(refs, 4, 1, geometry is not None)
    u_ref, halo_ref, w_ref, b_ref = ins
    (c_ref,) = outs
    hist_ref, part_ref = scratch
    step = pl.program_id(0)
    if host is not None:
        _hosted_begin(host, step, geometry)
    first = (step % tiles_per_seq) == 0
    hist_ref[0:HALO_ROWS, :] = jnp.where(first, 0.0, _round_bf16(halo_ref[...]))
    hist_ref[HALO_ROWS:HALO_ROWS + tm, :] = _round_bf16(u_ref[...])
    hist_ref[HALO_ROWS + tm:, :] = jnp.zeros((CONV_PAD_ROWS, D_MODEL), F32)
    lead = HALO_ROWS - CONV_HIST
    span = CONV_CHUNK_ROWS + SUBLANES
    for r0 in range(0, tm, CONV_CHUNK_ROWS):
        for c0 in range(0, D_MODEL, CONV_CHUNK_COLS):
            cs = slice(c0, c0 + CONV_CHUNK_COLS)
            for b in range(SUBLANES):
                part = jnp.zeros((span, CONV_CHUNK_COLS), F32)
                for a in range((CONV_WIDTH + lead + SUBLANES - 1) // SUBLANES):
                    j = SUBLANES * a + b - lead
                    if 0 <= j < CONV_WIDTH:
                        part = part + _round_bf16(w_ref[j:j + 1, cs]) * hist_ref[r0 + SUBLANES * a:
                                                                                  r0 + SUBLANES * a + span, cs]
                part_ref[b] = part
            acc = part_ref[0, 0:CONV_CHUNK_ROWS, :]
            for b in range(1, SUBLANES):
                acc = acc + part_ref[b, pl.ds(b, CONV_CHUNK_ROWS), :]
            c_ref[r0:r0 + CONV_CHUNK_ROWS, cs] = acc + b_ref[:, cs]
    if host is not None:
        _hosted_end(host, step, pl.num_programs(0), geometry)


def _prompt_conv(u, w_dw, b_dw, seq, hosted):
    tm = PROMPT_CONV_ROWS
    n = u.shape[0]
    halo_per_tile = tm // HALO_ROWS
    plan = _host_plan(*hosted, n // tm)
    kernel = functools.partial(_prompt_conv_kernel, tiles_per_seq=seq // tm, tm=tm, geometry=plan["geometry"])
    in_specs, out_specs, out_shapes, scratch, operands = _with_host(
        plan,
        [pl.BlockSpec((tm, D_MODEL), lambda i: (i, 0)),
         pl.BlockSpec((HALO_ROWS, D_MODEL), lambda i: (jnp.maximum(i * halo_per_tile - 1, 0), 0)),
         pl.BlockSpec(w_dw.shape, lambda i: (0, 0)),
         pl.BlockSpec(b_dw.shape, lambda i: (0, 0))],
        [pl.BlockSpec((tm, D_MODEL), lambda i: (i, 0))],
        [jax.ShapeDtypeStruct((n, D_MODEL), F32)],
        [pltpu.VMEM((HALO_ROWS + tm + CONV_PAD_ROWS, D_MODEL), F32),
         pltpu.VMEM((SUBLANES, CONV_CHUNK_ROWS + SUBLANES, CONV_CHUNK_COLS), F32)],
        [u, u, w_dw, b_dw])
    c_pre, shifted = pl.pallas_call(
        kernel,
        grid=(n // tm,),
        in_specs=in_specs,
        out_specs=out_specs,
        out_shape=out_shapes,
        scratch_shapes=scratch,
        compiler_params=pltpu.CompilerParams(dimension_semantics=("arbitrary",)),
        name="prompt_conv",
    )(*operands)
    return c_pre, shifted.reshape(plan["cache_shape"])


def _sample_conv_kernel(state_ref, u_ref, w_ref, b_ref, new_state_ref, c_ref):
    st = state_ref[...]
    u = u_ref[...]
    c = jnp.sum(st * w_ref[0:CONV_HIST, :][None], axis=1)
    c_ref[...] = c + w_ref[CONV_HIST:CONV_WIDTH, :] * u + b_ref[...]
    new_state_ref[:, 0:CONV_HIST - 1, :] = state_ref[:, 1:CONV_HIST, :]
    new_state_ref[:, CONV_HIST - 1, :] = u


def _sample_conv(state, u, w_dw, b_dw):
    bb = SAMPLE_CONV_SEQS
    nb = state.shape[0]
    return pl.pallas_call(
        _sample_conv_kernel,
        grid=(nb // bb,),
        in_specs=[pl.BlockSpec((bb, CONV_HIST, D_MODEL), lambda i: (i, 0, 0)),
                  pl.BlockSpec((bb, D_MODEL), lambda i: (i, 0)),
                  pl.BlockSpec(w_dw.shape, lambda i: (0, 0)),
                  pl.BlockSpec(b_dw.shape, lambda i: (0, 0))],
        out_specs=[pl.BlockSpec((bb, CONV_HIST, D_MODEL), lambda i: (i, 0, 0)),
                   pl.BlockSpec((bb, D_MODEL), lambda i: (i, 0))],
        out_shape=[jax.ShapeDtypeStruct(state.shape, F32), jax.ShapeDtypeStruct((nb, D_MODEL), F32)],
        compiler_params=pltpu.CompilerParams(dimension_semantics=("parallel",)),
        name="sample_conv",
    )(state, u, w_dw, b_dw)


def _sample_attn_kernel(q_ref, kn_ref, vn_ref, kv_ref, o_ref, lse_ref):
    scale = HEAD_DIM ** -0.5
    q = _round_bf16(q_ref[...])
    k = _round_bf16(kv_ref[:, :, 0])
    v = _round_bf16(kv_ref[:, :, 1])
    s = jnp.sum(k * q[:, None], axis=-1, keepdims=True) * scale
    sn = jnp.sum(_round_bf16(kn_ref[...]) * q, axis=-1, keepdims=True) * scale
    m = jnp.maximum(jnp.max(s, axis=1), sn)
    p = jnp.exp(s - m[:, None])
    pn = jnp.exp(sn - m)
    den = jnp.sum(p, axis=1) + pn
    o = (jnp.sum(_round_bf16(p / den[:, None]) * v, axis=1)
         + _round_bf16(pn / den) * _round_bf16(vn_ref[...]))
    o_ref[...] = o
    lse_ref[...] = jnp.broadcast_to(m + jnp.log(den), o.shape)


def _sample_attention(q, kn, vn, cache, g, window, dil):
    bb = SAMPLE_ATTN_SEQS
    n_seq = q.shape[0]
    strided = cache.reshape(n_seq, window // dil, dil, 2, HEADS_PER_GROUP, HEAD_DIM)
    head_spec = pl.BlockSpec((bb, HEADS_PER_GROUP, HEAD_DIM), lambda i: (i, 0, 0))
    return pl.pallas_call(
        _sample_attn_kernel,
        grid=(n_seq // bb,),
        in_specs=[head_spec, head_spec, head_spec,
                  pl.BlockSpec((bb, window // dil, None, 2, HEADS_PER_GROUP, HEAD_DIM),
                               lambda i: (i, 0, 0, 0, 0, 0))],
        out_specs=[head_spec, head_spec],
        out_shape=[jax.ShapeDtypeStruct(q.shape, F32), jax.ShapeDtypeStruct(q.shape, F32)],
        compiler_params=pltpu.CompilerParams(dimension_semantics=("parallel",)),
        name=f"sample_attn_g{g}",
    )(q, kn, vn, strided)


SHIFT_CHUNK_ROWS = 256


def _shift_geometry(n_seq, window):
    n_copy = n_seq * window - 1
    n_full = n_copy // SHIFT_CHUNK_ROWS
    return n_full, n_copy - n_full * SHIFT_CHUNK_ROWS


def _shift_in(src, buf_slot, sem, c):
    return pltpu.make_async_copy(src.at[pl.ds(c * SHIFT_CHUNK_ROWS + 1, SHIFT_CHUNK_ROWS)], buf_slot, sem)


def _shift_out(dst, buf_slot, sem, c):
    return pltpu.make_async_copy(buf_slot, dst.at[pl.ds(c * SHIFT_CHUNK_ROWS, SHIFT_CHUNK_ROWS)], sem)


def _shift_finish(src, fresh, dst, buf_slot, sem_a, sem_b, sem_fresh, *, window, n_seq):
    n_full, tail = _shift_geometry(n_seq, window)
    if tail:
        start = n_full * SHIFT_CHUNK_ROWS
        tail_in = pltpu.make_async_copy(src.at[pl.ds(start + 1, tail)], buf_slot.at[pl.ds(0, tail)], sem_a)
        tail_in.start()
        tail_in.wait()
        tail_out = pltpu.make_async_copy(buf_slot.at[pl.ds(0, tail)], dst.at[pl.ds(start, tail)], sem_b)
        tail_out.start()
        tail_out.wait()

    def fresh_copy(b):
        return pltpu.make_async_copy(fresh.at[b], dst.at[pl.ds(b * window + window - 1, 1)], sem_fresh)

    def start_fresh(b, carry):
        fresh_copy(b).start()
        return carry

    def wait_fresh(b, carry):
        fresh_copy(b).wait()
        return carry

    lax.fori_loop(0, n_seq, start_fresh, 0)
    lax.fori_loop(0, n_seq, wait_fresh, 0)


def _flat_cache(cache):
    _, n_seq, window = cache.shape[:3]
    return cache.reshape((n_seq * window,) + cache.shape[3:])


def _host_plan(cache, fresh, n_steps):
    _, n_seq, window = cache.shape[:3]
    flat = _flat_cache(cache)
    n_full, _ = _shift_geometry(n_seq, window)
    per_step = pl.cdiv(n_full, n_steps)
    buf_shape = (2, per_step, SHIFT_CHUNK_ROWS) + flat.shape[1:]
    any_spec = pl.BlockSpec(memory_space=pl.ANY)
    return dict(
        operands=[flat, fresh], in_specs=[any_spec, any_spec], out_spec=any_spec,
        out_shape=jax.ShapeDtypeStruct(flat.shape, flat.dtype),
        scratch=[pltpu.VMEM(buf_shape, flat.dtype), pltpu.SemaphoreType.DMA((2, per_step)),
                 pltpu.SemaphoreType.DMA((2, per_step)), pltpu.SemaphoreType.DMA((1,))],
        scratch_bytes=_nbytes(buf_shape, flat.dtype), geometry=dict(window=window, n_seq=n_seq),
        cache_shape=cache.shape)


def _hosted_shift_step(src, dst, buf, sem_in, sem_out, step, *, window, n_seq):
    n_full, _ = _shift_geometry(n_seq, window)
    per_step = buf.shape[1]
    cur = step % 2
    prv = 1 - cur
    for q in range(per_step):
        c = step * per_step + q
        c1 = c - per_step
        c2 = c - 2 * per_step

        @pl.when((c1 >= 0) & (c1 < n_full))
        def _():
            _shift_in(src, buf.at[prv, q], sem_in.at[prv, q], c1).wait()
            _shift_out(dst, buf.at[prv, q], sem_out.at[prv, q], c1).start()

        @pl.when((c2 >= 0) & (c2 < n_full))
        def _():
            _shift_out(dst, buf.at[cur, q], sem_out.at[cur, q], c2).wait()

        @pl.when(c < n_full)
        def _():
            _shift_in(src, buf.at[cur, q], sem_in.at[cur, q], c).start()


def _hosted_shift_drain(src, fresh, dst, buf, sem_in, sem_out, sem_fresh, step, *, window, n_seq):
    n_full, _ = _shift_geometry(n_seq, window)
    per_step = buf.shape[1]
    cur = step % 2
    prv = 1 - cur
    for q in range(per_step):
        c = step * per_step + q
        c1 = c - per_step

        @pl.when(c < n_full)
        def _():
            _shift_in(src, buf.at[cur, q], sem_in.at[cur, q], c).wait()
            _shift_out(dst, buf.at[cur, q], sem_out.at[cur, q], c).start()

        @pl.when((c1 >= 0) & (c1 < n_full))
        def _():
            _shift_out(dst, buf.at[prv, q], sem_out.at[prv, q], c1).wait()

        @pl.when(c < n_full)
        def _():
            _shift_out(dst, buf.at[cur, q], sem_out.at[cur, q], c).wait()

    _shift_finish(src, fresh, dst, buf.at[0, 0], sem_in.at[0, 0], sem_out.at[0, 0], sem_fresh.at[0],
                  window=window, n_seq=n_seq)


def _split_hosted(refs, n_in, n_out, hosting):
    if not hosting:
        return refs[:n_in], refs[n_in:n_in + n_out], refs[n_in + n_out:], None
    ins = refs[:n_in]
    src, fresh = refs[n_in:n_in + 2]
    outs = refs[n_in + 2:n_in + 2 + n_out]
    dst = refs[n_in + 2 + n_out]
    rest = refs[n_in + 3 + n_out:]
    buf, sem_in, sem_out, sem_fresh = rest[-4:]
    return ins, outs, rest[:-4], (src, fresh, dst, buf, sem_in, sem_out, sem_fresh)


def _hosted_begin(host, step, geometry):
    src, _, dst, buf, sem_in, sem_out, _ = host
    _hosted_shift_step(src, dst, buf, sem_in, sem_out, step, **geometry)


def _hosted_end(host, step, n_steps, geometry):
    @pl.when(step == n_steps - 1)
    def _():
        _hosted_shift_drain(*host, step, **geometry)


def _with_host(plan, in_specs, out_specs, out_shapes, scratch, operands):
    return (in_specs + plan["in_specs"], out_specs + [plan["out_spec"]], out_shapes + [plan["out_shape"]],
            scratch + plan["scratch"], operands + plan["operands"])


def _merge_kernel(x_ref, o0_ref, o1_ref, o2_ref, l0_ref, l1_ref, l2_ref, c_ref, sg_ref, cng_ref, cnb_ref,
                  wb_ref, wo_ref, gf_ref, wr_ref, br_ref, x1_ref, t_ref, cw_ref, order_ref):
    def token_order(ref, k):
        dil = ref.shape[0]
        if dil == 1:
            return ref[0]
        slabs = ref.shape[2] // LANES
        for r in range(dil):
            for s in range(slabs):
                order_ref[k, s, pl.ds(r, ref.shape[1], stride=dil), :] = ref[r, :, s * LANES:(s + 1) * LANES]
        return jnp.concatenate([order_ref[k, s] for s in range(slabs)], axis=-1)

    def per_head(packed):
        rows = packed.shape[0]
        return jnp.concatenate([jnp.broadcast_to(packed[:, h * LSE_LANES:h * LSE_LANES + 1], (rows, HEAD_DIM))
                                for h in range(HEADS_PER_GROUP)], axis=-1)

    l0, l1, l2 = token_order(l0_ref, 0), token_order(l1_ref, 1), token_order(l2_ref, 2)
    lm = jnp.maximum(jnp.maximum(l0, l1), l2)
    e0, e1, e2 = jnp.exp(l0 - lm), jnp.exp(l1 - lm), jnp.exp(l2 - lm)
    es = e0 + e1 + e2
    o_attn = per_head(e0 / es) * token_order(o0_ref, 3)
    o_attn = o_attn + per_head(e1 / es) * token_order(o1_ref, 4)
    o_attn = o_attn + per_head(e2 / es) * token_order(o2_ref, 5)

    cf = c_ref[...]
    mu = jnp.mean(cf, axis=-1, keepdims=True)
    var = jnp.mean(jnp.square(cf - mu), axis=-1, keepdims=True)
    cn = ((cf - mu) * lax.rsqrt(var + EPS)) * cng_ref[...] + cnb_ref[...]
    c = cn * _sigmoid(cn)

    ya = jnp.dot(o_attn.astype(BF16), wb_ref[0:GROUP_W, :], preferred_element_type=F32)
    yb = jnp.dot(c.astype(BF16), wb_ref[GROUP_W:, :], preferred_element_type=F32)
    mix = sg_ref[:, 0:D_MODEL] * ya + sg_ref[:, D_MODEL:] * yb
    x1 = x_ref[...] + jnp.dot(mix.astype(BF16), wo_ref[...], preferred_element_type=F32)
    x1_ref[...] = x1

    t = _rms_normed(x1, gf_ref[...]).astype(BF16)
    t_ref[...] = t
    logits = jnp.dot(t, wr_ref[...], preferred_element_type=F32) + br_ref[...]
    lane = lax.broadcasted_iota(jnp.int32, logits.shape, 1)
    lane_f = lane.astype(F32)
    big = float(LANES)
    is_group = (lane >= N_EXPERTS) & (lane < N_EXPERTS + N_COARSE)
    gl = jnp.where(is_group, logits, -jnp.inf)
    gmax = jnp.max(gl, axis=-1, keepdims=True)
    gsel = jnp.min(jnp.where(gl == gmax, lane_f, big), axis=-1, keepdims=True) - float(N_EXPERTS)
    gw = 1.0 / jnp.sum(jnp.where(is_group, jnp.exp(gl - gmax), 0.0), axis=-1, keepdims=True)
    in_group = (lane < N_EXPERTS) & ((lane // EXPERTS_PER_GROUP).astype(F32) == gsel)
    el = jnp.where(in_group, logits, -jnp.inf)
    v1 = jnp.max(el, axis=-1, keepdims=True)
    i1 = jnp.min(jnp.where(el == v1, lane_f, big), axis=-1, keepdims=True)
    el2 = jnp.where(lane_f == i1, -jnp.inf, el)
    v2 = jnp.max(el2, axis=-1, keepdims=True)
    i2 = jnp.min(jnp.where(el2 == v2, lane_f, big), axis=-1, keepdims=True)
    e2nd = jnp.exp(v2 - v1)
    tden = 1.0 + e2nd
    cw_ref[...] = jnp.where(lane_f == i1, (1.0 / tden) * gw, jnp.where(lane_f == i2, (e2nd / tden) * gw, 0.0))


def _merge(x, outs, lses, c_pre, sg, consts, batch, tm, tag):
    n = x.shape[0]
    tiles = n // (batch * tm)
    rows = [x, c_pre, sg]
    row_spec = lambda a: pl.BlockSpec((tm, a.shape[1]), lambda b, i: (b * tiles + i, 0))
    grp_spec = lambda a: pl.BlockSpec((None, a.shape[1], tm // a.shape[1], a.shape[3]), lambda b, i: (b, 0, i, 0))
    in_specs = ([row_spec(x)] + [grp_spec(a) for a in outs] + [grp_spec(a) for a in lses]
                + [row_spec(c_pre), row_spec(sg)] + [pl.BlockSpec(c.shape, lambda b, i: (0, 0)) for c in consts])
    out_shapes = [jax.ShapeDtypeStruct((n, D_MODEL), F32), jax.ShapeDtypeStruct((n, D_MODEL), BF16),
                  jax.ShapeDtypeStruct((n, LANES), F32)]
    out_specs = [row_spec(s) for s in out_shapes]
    block_bytes = sum(_nbytes((tm, a.shape[1]), a.dtype) for a in rows + out_shapes)
    block_bytes += 6 * _nbytes((tm, GROUP_W), F32) + sum(_nbytes(c.shape, c.dtype) for c in consts)
    return pl.pallas_call(
        _merge_kernel,
        grid=(batch, tiles),
        in_specs=in_specs,
        out_specs=out_specs,
        out_shape=out_shapes,
        scratch_shapes=[pltpu.VMEM((2 * N_GROUPS, HEADS_PER_GROUP, tm, HEAD_DIM), F32)],
        compiler_params=pltpu.CompilerParams(
            dimension_semantics=("parallel", "parallel"),
            vmem_limit_bytes=_vmem_limit(block_bytes, _nbytes((2 * N_GROUPS, tm, GROUP_W), F32))),
        name="merge_" + tag,
    )(x, *outs, *lses, c_pre, sg, *consts)


def _moe_kernel(*refs, geometry):
    ins, outs, _, host = _split_hosted(refs, 7, 1, geometry is not None)
    x1_ref, t_ref, cw_ref, wg_ref, wu_ref, wd_ref, gn_ref = ins
    (y_ref,) = outs
    e = pl.program_id(1)
    n_e = pl.num_programs(1)
    step = pl.program_id(0) * n_e + e
    if host is not None:
        _hosted_begin(host, step, geometry)

    t = t_ref[...]
    gate = jnp.dot(t, wg_ref[...], preferred_element_type=F32)
    up = jnp.dot(t, wu_ref[...], preferred_element_type=F32)
    lane = lax.broadcasted_iota(jnp.int32, cw_ref.shape, 1)
    w = jnp.sum(jnp.where(lane == e, cw_ref[...], 0.0), axis=-1, keepdims=True)
    h = ((gate * _sigmoid(gate)) * up) * w
    part = jnp.dot(h.astype(BF16), wd_ref[...], preferred_element_type=F32)

    @pl.when(e == 0)
    def _():
        y_ref[...] = part

    @pl.when(e > 0)
    def _():
        y_ref[...] += part

    @pl.when(e == n_e - 1)
    def _():
        y_ref[...] = _rms_normed(x1_ref[...] + y_ref[...], gn_ref[...])

    if host is not None:
        _hosted_end(host, step, pl.num_programs(0) * n_e, geometry)


def _moe_final(x1, t, cw, w_gate, w_up, w_down, g_final, tm, tag, hosted=None):
    n = x1.shape[0]
    grid = (n // tm, N_EXPERTS)
    row = lambda w: pl.BlockSpec((tm, w), lambda i, e: (i, 0))
    once = lambda w: pl.BlockSpec((tm, w), lambda i, e: (i, 0), pipeline_mode=pl.Buffered(1))
    in_specs = [once(D_MODEL), once(D_MODEL), once(LANES),
                pl.BlockSpec((None, D_MODEL, D_EXPERT), lambda i, e: (e, 0, 0)),
                pl.BlockSpec((None, D_MODEL, D_EXPERT), lambda i, e: (e, 0, 0)),
                pl.BlockSpec((None, D_EXPERT, D_MODEL), lambda i, e: (e, 0, 0)),
                pl.BlockSpec(g_final.shape, lambda i, e: (0, 0))]
    out_specs = [row(D_MODEL)]
    out_shapes = [jax.ShapeDtypeStruct((n, D_MODEL), F32)]
    scratch = []
    operands = [x1, t, cw, w_gate, w_up, w_down, g_final]
    scratch_bytes = 0
    geometry = None
    if hosted is not None:
        plan = _host_plan(*hosted, grid[0] * grid[1])
        geometry = plan["geometry"]
        in_specs += plan["in_specs"]
        out_specs.append(plan["out_spec"])
        out_shapes.append(plan["out_shape"])
        scratch += plan["scratch"]
        operands += plan["operands"]
        scratch_bytes += plan["scratch_bytes"]
    block_bytes = (_nbytes((tm, D_MODEL), F32) * 2 + _nbytes((tm, D_MODEL), BF16) + _nbytes((tm, LANES), F32)
                   + 3 * _nbytes((D_MODEL, D_EXPERT), BF16))
    outs = pl.pallas_call(
        functools.partial(_moe_kernel, geometry=geometry),
        grid=grid,
        in_specs=in_specs,
        out_specs=out_specs,
        out_shape=out_shapes,
        scratch_shapes=scratch,
        compiler_params=pltpu.CompilerParams(
            dimension_semantics=("arbitrary", "arbitrary"),
            vmem_limit_bytes=_vmem_limit(block_bytes, scratch_bytes)),
        name="moe_" + tag,
    )(*operands)
    if hosted is None:
        return outs[0]
    return outs[0], outs[1].reshape(hosted[0].shape)


def kernel(x_prompt, x_sample, cache_kv_w128, cache_kv_w512, cache_kv_w2048, state_conv, norm_mix, norm_ffn,
           norm_final, w_in, w_dw, b_dw, conv_norm_g, conv_norm_b, w_branch, w_out, w_route_group,
           b_route_group, w_route_expert, b_route_expert, w_exp_gate, w_exp_up, w_exp_down):
    depth = w_in.shape[0]
    assert depth == 1, "single-layer step"
    batch, seq, _ = x_prompt.shape
    n_seq, dec_seq, _ = x_sample.shape
    assert dec_seq == 1
    n_prompt = batch * seq

    w_in_b = w_in[0].astype(BF16)
    w_qkv = w_in_b[:, :3 * ATTN_W]
    w_glu_gate = w_in_b[:, 3 * ATTN_W:]
    g_mix = norm_mix[0][None]
    g_ffn = norm_ffn[0][None]
    g_final = norm_final[None]
    w_dw0, b_dw0 = w_dw[0], b_dw[0][None]
    w_router = jnp.concatenate([w_route_expert[0], w_route_group[0]], axis=1)
    w_router = jnp.pad(w_router, ((0, 0), (0, LANES - w_router.shape[1]))).astype(BF16)
    b_router = jnp.concatenate([b_route_expert[0], b_route_group[0]])
    b_router = jnp.pad(b_router, (0, LANES - b_router.shape[0]))[None]
    merge_consts = (conv_norm_g[0][None], conv_norm_b[0][None], w_branch[0].astype(BF16), w_out[0].astype(BF16),
                    g_ffn, w_router, b_router)
    moe_w = (w_exp_gate[0].astype(BF16), w_exp_up[0].astype(BF16), w_exp_down[0].astype(BF16))
    caches = (cache_kv_w128, cache_kv_w512, cache_kv_w2048)

    xs = x_sample.reshape(n_seq, D_MODEL)
    (qkvf,) = _row_call(_qkv_sample_kernel, xs, (g_mix, w_qkv), [jax.ShapeDtypeStruct((n_seq, 3 * ATTN_W), F32)],
                        n_seq, "proj_qkv_s")
    u_s, sg_s = _glu_and_gates(xs, g_mix, w_glu_gate, n_seq, "s")
    qkv5 = qkvf.reshape(n_seq, 3, N_GROUPS, HEADS_PER_GROUP, HEAD_DIM)
    outs_s, lses_s, fresh = [], [], []
    for g, (win, dil) in enumerate(DILATED_GROUPS):
        q, kn, vn = qkv5[:, 0, g], qkv5[:, 1, g], qkv5[:, 2, g]
        o, l = _sample_attention(q, kn, vn, caches[g], g, win, dil)
        outs_s.append(o.reshape(1, 1, n_seq, GROUP_W))
        lses_s.append(l[:, :, :LSE_LANES].reshape(1, 1, n_seq, LSE_W))
        fresh.append(jnp.stack([kn, vn], axis=1)[:, None])
    kv_s = [None] * N_GROUPS

    xp = x_prompt.reshape(n_prompt, D_MODEL)
    a0, a1, a2, t0, t1, t2 = _qkv_prompt(xp, g_mix, w_qkv, batch, seq)
    u_p, sg_p = _glu_and_gates(xp, g_mix, w_glu_gate, PROMPT_PROJ_ROWS, "p")
    o0, l0, kv_s[0] = _prompt_attention(a0, 0, hosted=(caches[0], fresh[0]))
    o1, l1 = _prompt_attention(a1, 1)
    o2, l2 = _prompt_attention(a2, 2)
    c_pre, kv_s[1] = _prompt_conv(u_p, w_dw0, b_dw0, seq, hosted=(caches[1], fresh[1]))
    x1, t, cw = _merge(xp, (o0, o1, o2), (l0, l1, l2), c_pre, sg_p, merge_consts, batch, PROMPT_MERGE_ROWS, "p")
    y_prompt, kv_s[2] = _moe_final(x1, t, cw, *moe_w, g_final, PROMPT_MOE_ROWS, "p",
                                   hosted=(caches[2], fresh[2]))
    y_prompt = y_prompt.reshape(batch, seq, D_MODEL)
    kv_p = [tail.reshape(batch, tail.shape[1], 2, HEADS_PER_GROUP, HEAD_DIM)[None] for tail in (t0, t1, t2)]
    conv_p = u_p.reshape(batch, seq, D_MODEL)[:, seq - CONV_HIST:][None]

    conv_s, c_pre_s = _sample_conv(state_conv[0], u_s, w_dw0, b_dw0)
    x1s, ts, cws = _merge(xs, outs_s, lses_s, c_pre_s, sg_s, merge_consts, 1, n_seq, "s")
    y_sample = _moe_final(x1s, ts, cws, *moe_w, g_final, n_seq, "s").reshape(n_seq, 1, D_MODEL)

    return (y_prompt, y_sample, kv_p[0], kv_p[1], kv_p[2], conv_p, kv_s[0], kv_s[1], kv_s[2], conv_s[None])
```
